```python
import math
import jax, jax.numpy as jnp
from jax import lax
import numpy as np

D_MODEL = 2048
BATCH = 1
SEQ = 16384
DEPTH = 2

SGU_CHUNK = 128
SGU_GROUPS = 8
SGU_HEAD = 128
SGU_WIDTH = SGU_GROUPS * SGU_HEAD
MLA_HEADS = 8
MLA_Q_RANK = 512
MLA_KV_RANK = 512
MLA_NOPE = 128
MLA_ROPE = 64
MLA_V = 128
ROPE_THETA = 10000.0
ATTN_BLOCK = 128
HYB_IN = 2 * SGU_WIDTH + MLA_Q_RANK + MLA_KV_RANK + MLA_ROPE
HYB_MIX = SGU_WIDTH + MLA_HEADS * MLA_V
SSM_INNER = 2 * D_MODEL
SSM_HEAD_DIM = 64
SSM_HEADS = SSM_INNER // SSM_HEAD_DIM
SSM_GROUPS = 8
SSM_STATE = 128
SSM_CONV = 5
SSM_CHUNK = 256
SSM_CONV_CH = SSM_INNER + 2 * SSM_GROUPS * SSM_STATE
SSM_IN = SSM_INNER + SSM_CONV_CH + 2 * SSM_HEADS
FFN_HIDDEN = 4 * D_MODEL
N_EVEN = (DEPTH + 1) // 2
N_ODD = DEPTH // 2
EPS = 1e-6

kernel_name = 'hybrid_gmlp_mla_mamba2_adaln_encoder'


def rmsnorm(x, g):
    xf = x.astype(jnp.float32)
    y = xf * lax.rsqrt(jnp.mean(xf * xf, axis=-1, keepdims=True) + EPS)
    return (y * g.astype(jnp.float32)).astype(x.dtype)


def layernorm(x, g):
    xf = x.astype(jnp.float32)
    xc = xf - jnp.mean(xf, axis=-1, keepdims=True)
    y = xc * lax.rsqrt(jnp.mean(xc * xc, axis=-1, keepdims=True) + EPS)
    return (y * g.astype(jnp.float32)).astype(x.dtype)


def rope(x, cos, sin):
    half = x.shape[-1] // 2
    xf = x.astype(jnp.float32)
    x1, x2 = xf[..., :half], xf[..., half:]
    return jnp.concatenate([x1 * cos - x2 * sin, x2 * cos + x1 * sin], axis=-1).astype(x.dtype)


def gmlp_sgu(u, v, norm_g, w_s, b_s):
    b, s, _ = v.shape
    v = layernorm(v, norm_g)
    vc = v.reshape(b, s // SGU_CHUNK, SGU_CHUNK, SGU_GROUPS, SGU_HEAD)
    mixed = jnp.einsum('gij,bcjgd->bcigd', w_s, vc) + b_s.T[None, None, :, :, None]
    return u * mixed.reshape(b, s, SGU_WIDTH)


def mla_attention(q_lat, kv_lat, k_pe, positions, q_norm_g, kv_norm_g, w_uq, w_ukv):
    b, s, _ = q_lat.shape
    q = (rmsnorm(q_lat, q_norm_g) @ w_uq).reshape(b, s, MLA_HEADS, MLA_NOPE + MLA_ROPE)
    kv = (rmsnorm(kv_lat, kv_norm_g) @ w_ukv).reshape(b, s, MLA_HEADS, MLA_NOPE + MLA_V)
    q_nope, q_pe = q[..., :MLA_NOPE], q[..., MLA_NOPE:]
    k_nope, v = kv[..., :MLA_NOPE], kv[..., MLA_NOPE:]
    half = MLA_ROPE // 2
    inv_freq = ROPE_THETA ** (-jnp.arange(half, dtype=jnp.float32) / half)
    ang = positions.astype(jnp.float32)[..., None] * inv_freq
    cos, sin = jnp.cos(ang), jnp.sin(ang)
    q_pe = rope(q_pe, cos[:, :, None], sin[:, :, None])
    k_pe = rope(k_pe, cos, sin)
    scale = (MLA_NOPE + MLA_ROPE) ** -0.5
    nb = s // ATTN_BLOCK
    qn_b = q_nope.reshape(b, nb, ATTN_BLOCK, MLA_HEADS, MLA_NOPE).transpose(1, 0, 2, 3, 4)
    qp_b = q_pe.reshape(b, nb, ATTN_BLOCK, MLA_HEADS, MLA_ROPE).transpose(1, 0, 2, 3, 4)

    def attend(blk):
        qn, qp = blk
        sc = (jnp.einsum('bqhd,bkhd->bhqk', qn, k_nope, preferred_element_type=jnp.float32)
              + jnp.einsum('bqhr,bkr->bhqk', qp, k_pe, preferred_element_type=jnp.float32))
        p = jax.nn.softmax(sc * scale, axis=-1).astype(v.dtype)
        return jnp.einsum('bhqk,bkhd->bqhd', p, v)

    o = lax.map(attend, (qn_b, qp_b))
    return o.transpose(1, 0, 2, 3, 4).reshape(b, s, MLA_HEADS * MLA_V)


def hybrid_mixer(h, positions, w_in, sgu_norm_g, sgu_w, sgu_b, q_norm_g, kv_norm_g, w_uq, w_ukv, w_out):
    proj = h @ w_in
    cuts = [SGU_WIDTH, 2 * SGU_WIDTH, 2 * SGU_WIDTH + MLA_Q_RANK,
            2 * SGU_WIDTH + MLA_Q_RANK + MLA_KV_RANK]
    u, v, q_lat, kv_lat, k_pe = jnp.split(proj, cuts, axis=-1)
    a_out = gmlp_sgu(jax.nn.gelu(u), jax.nn.gelu(v), sgu_norm_g, sgu_w, sgu_b)
    b_out = mla_attention(q_lat, kv_lat, k_pe, positions, q_norm_g, kv_norm_g, w_uq, w_ukv)
    return jnp.concatenate([a_out, b_out], axis=-1) @ w_out


def depthwise_conv_centred(x, w, bias):
    ch = x.shape[-1]
    y = lax.conv_general_dilated(x, w[:, None, :].astype(x.dtype), window_strides=(1,),
                                 padding=[(SSM_CONV // 2, SSM_CONV // 2)],
                                 dimension_numbers=('NWC', 'WIO', 'NWC'),
                                 feature_group_count=ch)
    return y + bias


def ssd_scan(x, dt, a, bm, cm):
    b, s = x.shape[:2]
    pad = (-s) % SSM_CHUNK
    if pad:
        padw = lambda t: jnp.pad(t, [(0, 0), (0, pad)] + [(0, 0)] * (t.ndim - 2))
        x, dt, bm, cm = padw(x), padw(dt), padw(bm), padw(cm)
    sp = s + pad
    nc = sp // SSM_CHUNK
    hpg = SSM_HEADS // SSM_GROUPS

    def chunks(t):
        return jnp.moveaxis(t.reshape(b, nc, SSM_CHUNK, *t.shape[2:]), 1, 0)

    xs = chunks(x.reshape(b, sp, SSM_GROUPS, hpg, SSM_HEAD_DIM))
    dts = chunks(dt.reshape(b, sp, SSM_GROUPS, hpg))
    bs, cs = chunks(bm), chunks(cm)
    a_g = a.reshape(SSM_GROUPS, hpg)
    mask = jnp.tril(jnp.ones((SSM_CHUNK, SSM_CHUNK), dtype=bool))[None, :, :, None, None]

    def step(state, inp):
        xc, dtc, bc, cc = inp
        cum = jnp.cumsum(dtc * a_g, axis=1)
        seg = cum[:, :, None] - cum[:, None, :]
        decay = jnp.exp(jnp.where(mask, seg, -jnp.inf))
        cb = jnp.einsum('bign,bjgn->bijg', cc, bc)
        w = cb[..., None] * decay * dtc[:, None]
        y = jnp.einsum('bijgh,bjghp->bighp', w, xc)
        y = y + jnp.einsum('bign,bghpn->bighp', cc, state) * jnp.exp(cum)[..., None]
        to_end = jnp.exp(cum[:, -1:] - cum) * dtc
        state = (state * jnp.exp(cum[:, -1])[..., None, None]
                 + jnp.einsum('bjgh,bjgn,bjghp->bghpn', to_end, bc, xc))
        return state, y

    state0 = jnp.zeros((b, SSM_GROUPS, hpg, SSM_HEAD_DIM, SSM_STATE), jnp.float32)
    _, ys = lax.scan(step, state0, (xs, dts, bs, cs))
    return jnp.moveaxis(ys, 0, 1).reshape(b, sp, SSM_HEADS, SSM_HEAD_DIM)[:, :s]


def mamba2_bidirectional(h, w_in, conv_w, conv_b, dt_bias, a_log, d_skip, norm_g, w_out):
    b, s, _ = h.shape
    proj = h @ w_in
    z, xbc, dt = jnp.split(proj, [SSM_INNER, SSM_INNER + SSM_CONV_CH], axis=-1)
    xbc = jax.nn.silu(depthwise_conv_centred(xbc, conv_w, conv_b))
    xs, bm, cm = jnp.split(xbc, [SSM_INNER, SSM_INNER + SSM_GROUPS * SSM_STATE], axis=-1)
    f32 = jnp.float32
    xs = xs.reshape(b, s, SSM_HEADS, SSM_HEAD_DIM).astype(f32)
    bm = bm.reshape(b, s, SSM_GROUPS, SSM_STATE).astype(f32)
    cm = cm.reshape(b, s, SSM_GROUPS, SSM_STATE).astype(f32)
    dt = jax.nn.softplus(dt.astype(f32) + dt_bias.astype(f32).reshape(2 * SSM_HEADS))
    dt_f, dt_b = dt[..., :SSM_HEADS], dt[..., SSM_HEADS:]
    a = -jnp.exp(a_log.astype(f32))
    flip = lambda t: jnp.flip(t, axis=1)
    y_f = ssd_scan(xs, dt_f, a[0], bm, cm)
    y_b = flip(ssd_scan(flip(xs), flip(dt_b), a[1], flip(bm), flip(cm)))
    y = y_f + y_b + d_skip.astype(f32)[:, None] * xs
    y = y.reshape(b, s, SSM_INNER) * jax.nn.silu(z.astype(f32))
    yg = y.reshape(b, s, SSM_GROUPS, SSM_INNER // SSM_GROUPS)
    yg = yg * lax.rsqrt(jnp.mean(yg * yg, axis=-1, keepdims=True) + EPS)
    y = (yg.reshape(b, s, SSM_INNER) * norm_g.astype(f32)).astype(h.dtype)
    return y @ w_out


def squared_relu_mlp(h, w1, w2):
    return jnp.square(jax.nn.relu(h @ w1)) @ w2


def setup_inputs(seed: int = 0) -> dict:
    key = jax.random.key(seed)
    ks = jax.random.split(key, 32)
    f32 = jnp.float32

    def nrm(k, shape, scale):
        return jax.random.normal(k, shape, f32) * scale

    def gain(k, shape):
        return 1.0 + 0.02 * jax.random.normal(k, shape, f32)

    x = nrm(ks[0], (BATCH, SEQ, D_MODEL), 1.0)
    c = nrm(ks[1], (BATCH, D_MODEL), 1.0)
    offset = jax.random.randint(ks[2], (BATCH, 1), 0, 1024, dtype=jnp.int32)
    positions = jnp.arange(SEQ, dtype=jnp.int32)[None, :] + offset
    dt0 = jnp.exp(jax.random.uniform(ks[21], (N_ODD, 2, SSM_HEADS), f32,
                                     math.log(1e-3), math.log(1e-1)))
    return {
        'x': x,
        'c': c,
        'positions': positions,
        'ada_w': nrm(ks[3], (DEPTH, D_MODEL, 6 * D_MODEL), 0.5 * D_MODEL ** -0.5),
        'ada_b': nrm(ks[4], (DEPTH, 6 * D_MODEL), 0.02),
        'norm_mix_g': gain(ks[5], (DEPTH, D_MODEL)),
        'norm_ffn_g': gain(ks[6], (DEPTH, D_MODEL)),
        'ffn_w1': nrm(ks[7], (DEPTH, D_MODEL, FFN_HIDDEN), D_MODEL ** -0.5),
        'ffn_w2': nrm(ks[8], (DEPTH, FFN_HIDDEN, D_MODEL), FFN_HIDDEN ** -0.5),
        'hyb_w_in': nrm(ks[9], (N_EVEN, D_MODEL, HYB_IN), D_MODEL ** -0.5),
        'sgu_norm_g': gain(ks[10], (N_EVEN, SGU_WIDTH)),
        'sgu_w': nrm(ks[11], (N_EVEN, SGU_GROUPS, SGU_CHUNK, SGU_CHUNK), SGU_CHUNK ** -0.5),
        'sgu_b': gain(ks[12], (N_EVEN, SGU_GROUPS, SGU_CHUNK)),
        'mla_q_norm_g': gain(ks[13], (N_EVEN, MLA_Q_RANK)),
        'mla_kv_norm_g': gain(ks[14], (N_EVEN, MLA_KV_RANK)),
        'mla_w_uq': nrm(ks[15], (N_EVEN, MLA_Q_RANK, MLA_HEADS * (MLA_NOPE + MLA_ROPE)), MLA_Q_RANK ** -0.5),
        'mla_w_ukv': nrm(ks[16], (N_EVEN, MLA_KV_RANK, MLA_HEADS * (MLA_NOPE + MLA_V)), MLA_KV_RANK ** -0.5),
        'hyb_w_out': nrm(ks[17], (N_EVEN, HYB_MIX, D_MODEL), HYB_MIX ** -0.5),
        'ssm_w_in': nrm(ks[18], (N_ODD, D_MODEL, SSM_IN), D_MODEL ** -0.5),
        'ssm_conv_w': nrm(ks[19], (N_ODD, SSM_CONV, SSM_CONV_CH), SSM_CONV ** -0.5),
        'ssm_conv_b': nrm(ks[20], (N_ODD, SSM_CONV_CH), 0.02),
        'ssm_dt_bias': dt0 + jnp.log(-jnp.expm1(-dt0)),
        'ssm_a_log': jnp.log(jax.random.uniform(ks[22], (N_ODD, 2, SSM_HEADS), f32, 1.0, 16.0)),
        'ssm_d': 1.0 + 0.1 * jax.random.normal(ks[23], (N_ODD, SSM_HEADS), f32),
        'ssm_norm_g': gain(ks[24], (N_ODD, SSM_INNER)),
        'ssm_w_out': nrm(ks[25], (N_ODD, SSM_INNER, D_MODEL), SSM_INNER ** -0.5),
        'final_norm_g': gain(ks[26], (D_MODEL,)),
    }


def reference(x, c, positions, ada_w, ada_b, norm_mix_g, norm_ffn_g, ffn_w1, ffn_w2,
              hyb_w_in, sgu_norm_g, sgu_w, sgu_b, mla_q_norm_g, mla_kv_norm_g, mla_w_uq,
              mla_w_ukv, hyb_w_out, ssm_w_in, ssm_conv_w, ssm_conv_b, ssm_dt_bias,
              ssm_a_log, ssm_d, ssm_norm_g, ssm_w_out, final_norm_g):
    cond = jax.nn.silu(c)
    for l in range(DEPTH):
        mod = (cond @ ada_w[l] + ada_b[l])[:, None, :]
        sh1, sc1, g1, sh2, sc2, g2 = jnp.split(mod, 6, axis=-1)
        h = rmsnorm(x, norm_mix_g[l]) * (1 + sc1) + sh1
        i = l // 2
        if l % 2 == 0:
            m = hybrid_mixer(h, positions, hyb_w_in[i], sgu_norm_g[i], sgu_w[i], sgu_b[i],
                             mla_q_norm_g[i], mla_kv_norm_g[i], mla_w_uq[i], mla_w_ukv[i],
                             hyb_w_out[i])
        else:
            m = mamba2_bidirectional(h, ssm_w_in[i], ssm_conv_w[i], ssm_conv_b[i],
                                     ssm_dt_bias[i], ssm_a_log[i], ssm_d[i],
                                     ssm_norm_g[i], ssm_w_out[i])
        x = x + g1 * m
        h = rmsnorm(x, norm_ffn_g[l]) * (1 + sc2) + sh2
        x = x + g2 * squared_relu_mlp(h, ffn_w1[l], ffn_w2[l])
    return rmsnorm(x, final_norm_g)
```

```python
import functools
import math

import jax
import jax.numpy as jnp
from jax import lax
from jax.experimental import pallas as pl
from jax.experimental.pallas import tpu as pltpu

F32 = jnp.float32
BF16 = jnp.bfloat16

V7X_LANES = 128
V7X_BF16_SUBLANE_TILE = 16
V7X_VMEM_BYTES = 64 * 1024 * 1024

D_MODEL = 2048
SGU_CHUNK = 128
SGU_GROUPS = 8
SGU_HEAD = 128
SGU_WIDTH = SGU_GROUPS * SGU_HEAD
MLA_HEADS = 8
MLA_Q_RANK = 512
MLA_KV_RANK = 512
MLA_NOPE = 128
MLA_ROPE = 64
MLA_V = 128
ROPE_THETA = 10000.0
HYB_MIX = SGU_WIDTH + MLA_HEADS * MLA_V
SSM_INNER = 2 * D_MODEL
SSM_HEAD_DIM = 64
SSM_HEADS = SSM_INNER // SSM_HEAD_DIM
SSM_GROUPS = 8
SSM_STATE = 128
SSM_CONV = 5
SSM_CHUNK = 256
SSM_CONV_CH = SSM_INNER + 2 * SSM_GROUPS * SSM_STATE
FFN_HIDDEN = 4 * D_MODEL
EPS = 1e-6

QK_PAD = 2 * V7X_LANES
HEADS_PER_GROUP = SSM_HEADS // SSM_GROUPS
PAIRS_PER_GROUP = HEADS_PER_GROUP * SSM_HEAD_DIM // V7X_LANES
X_BLOCKS = SSM_INNER // V7X_LANES
XBC_BLOCKS = SSM_CONV_CH // V7X_LANES
ZX_WIDTH = SSM_INNER + SSM_CONV_CH
LOG2E = 1.4426950408889634


def _cparams(sem, vmem_mib):
    return pltpu.CompilerParams(dimension_semantics=sem,
                                vmem_limit_bytes=vmem_mib * 1024 * 1024)


def _const_spec(shape):
    nd = len(shape)
    return pl.BlockSpec(shape, lambda *_: (0,) * nd, pipeline_mode=pl.Buffered(1))


def _norm_mod(x, g, sc, sh):
    y = x * lax.rsqrt(jnp.mean(x * x, axis=-1, keepdims=True) + EPS)
    return (y * g) * (1.0 + sc) + sh


def _silu(x):
    return x * jax.nn.sigmoid(x)


def _ada_kernel(c_ref, w_ref, b_ref, o_ref):
    c = c_ref[...]
    cond = jnp.broadcast_to(_silu(c), (8, c.shape[-1]))
    r = jnp.dot(cond, w_ref[0], preferred_element_type=F32,
                precision=lax.Precision.HIGHEST)
    o_ref[0] = r[0:1] + b_ref[0]


def _ada_mod(c, ada_w, ada_b):
    depth, d, n = ada_w.shape
    tn = 1024
    return pl.pallas_call(
        _ada_kernel,
        grid=(depth, n // tn),
        in_specs=[pl.BlockSpec((1, d), lambda l, j: (0, 0)),
                  pl.BlockSpec((1, d, tn), lambda l, j: (l, 0, j)),
                  pl.BlockSpec((1, 1, tn), lambda l, j: (l, 0, j))],
        out_specs=pl.BlockSpec((1, 1, tn), lambda l, j: (l, 0, j)),
        out_shape=jax.ShapeDtypeStruct((depth, 1, n), F32),
        compiler_params=_cparams(("parallel", "parallel"), 40),
        name="ada_mod",
    )(c, ada_w, ada_b.reshape(depth, 1, n))


def _rope_kernel(pos_ref, inv_ref, cos_ref, sin_ref):
    ang = pos_ref[...] * inv_ref[...]
    cos_ref[...] = jnp.cos(ang)
    sin_ref[...] = jnp.sin(ang)


def _rope_tables(pos_col, inv128):
    s = pos_col.shape[0]
    tm = min(s, 2048)
    return pl.pallas_call(
        _rope_kernel,
        grid=(s // tm,),
        in_specs=[pl.BlockSpec((tm, 1), lambda i: (i, 0)),
                  pl.BlockSpec((1, V7X_LANES), lambda i: (0, 0))],
        out_specs=[pl.BlockSpec((tm, V7X_LANES), lambda i: (i, 0))] * 2,
        out_shape=[jax.ShapeDtypeStruct((s, V7X_LANES), F32)] * 2,
        compiler_params=_cparams(("parallel",), 32),
        name="rope_tables",
    )(pos_col, inv128)


def _hyb_in_kernel(x_ref, g_ref, sc_ref, sh_ref, win_ref, lng_ref, sw_ref, sb_ref,
                   qg_ref, kvg_ref, wqa_ref, wqb_ref, wk_ref, wv_ref, cos_ref, sin_ref,
                   a_ref, q_ref, k_ref, v_ref, proj_scr, vn_scr, *, q_scale):
    tm = x_ref.shape[0]
    h = _norm_mod(x_ref[...], g_ref[...], sc_ref[...], sh_ref[...])
    proj_scr[...] = jnp.dot(h.astype(BF16), win_ref[...], preferred_element_type=F32)

    v = jax.nn.gelu(proj_scr[:, SGU_WIDTH:2 * SGU_WIDTH])
    vc = v - jnp.mean(v, axis=-1, keepdims=True)
    vn = vc * lax.rsqrt(jnp.mean(vc * vc, axis=-1, keepdims=True) + EPS) * lng_ref[...]
    vn_scr[...] = vn.astype(BF16)
    for c in range(tm // SGU_CHUNK):
        rows = slice(c * SGU_CHUNK, (c + 1) * SGU_CHUNK)
        for g in range(SGU_GROUPS):
            cols = slice(g * SGU_HEAD, (g + 1) * SGU_HEAD)
            mixed = jnp.dot(sw_ref[g], vn_scr[rows, cols], preferred_element_type=F32)
            u = jax.nn.gelu(proj_scr[rows, cols])
            a_ref[rows, cols] = (u * (mixed + sb_ref[:, cols])).astype(BF16)

    cos = cos_ref[...]
    sin = sin_ref[...]
    lat0 = 2 * SGU_WIDTH
    ql = proj_scr[:, lat0:lat0 + MLA_Q_RANK]
    qn = (ql * lax.rsqrt(jnp.mean(ql * ql, axis=-1, keepdims=True) + EPS) * qg_ref[...]).astype(BF16)
    qa = jnp.dot(qn, wqa_ref[...], preferred_element_type=F32)
    qb = jnp.dot(qn, wqb_ref[...], preferred_element_type=F32)
    for hd in range(MLA_HEADS):
        o = hd * QK_PAD
        q_ref[:, o:o + V7X_LANES] = (qa[:, o:o + V7X_LANES] * q_scale).astype(BF16)
        pe = (qa[:, o + V7X_LANES:o + QK_PAD] * cos
              + qb[:, hd * V7X_LANES:(hd + 1) * V7X_LANES] * sin)
        q_ref[:, o + V7X_LANES:o + QK_PAD] = (pe * q_scale).astype(BF16)

    kv0 = lat0 + MLA_Q_RANK
    kvl = proj_scr[:, kv0:kv0 + MLA_KV_RANK]
    kvn = (kvl * lax.rsqrt(jnp.mean(kvl * kvl, axis=-1, keepdims=True) + EPS) * kvg_ref[...]).astype(BF16)
    kn = jnp.dot(kvn, wk_ref[...], preferred_element_type=F32)
    vv = jnp.dot(kvn, wv_ref[...], preferred_element_type=F32)
    pe0 = kv0 + MLA_KV_RANK
    kpe = (proj_scr[:, pe0:pe0 + V7X_LANES] * cos
           + proj_scr[:, pe0 + V7X_LANES:pe0 + 2 * V7X_LANES] * sin).astype(BF16)
    ones = jnp.ones((tm, V7X_LANES), BF16)
    for hd in range(MLA_HEADS):
        o = hd * QK_PAD
        hs = slice(hd * V7X_LANES, (hd + 1) * V7X_LANES)
        k_ref[:, o:o + V7X_LANES] = kn[:, hs].astype(BF16)
        k_ref[:, o + V7X_LANES:o + QK_PAD] = kpe
        v_ref[:, o:o + V7X_LANES] = vv[:, hs].astype(BF16)
        v_ref[:, o + V7X_LANES:o + QK_PAD] = ones


def _hyb_in(x2, g, sc, sh, win, lng, sw, sb, qg, kvg, wqa, wqb, wk, wv, cos, sin, q_scale):
    s, d = x2.shape
    tm = min(s, 512)
    nproj = win.shape[1]
    row = lambda w: pl.BlockSpec((tm, w), lambda i: (i, 0))
    vec = lambda w: pl.BlockSpec((1, w), lambda i: (0, 0))
    hq = MLA_HEADS * QK_PAD
    return pl.pallas_call(
        functools.partial(_hyb_in_kernel, q_scale=q_scale),
        grid=(s // tm,),
        in_specs=[row(d), vec(d), vec(d), vec(d), _const_spec(win.shape), vec(SGU_WIDTH),
                  _const_spec(sw.shape), _const_spec(sb.shape), vec(MLA_Q_RANK), vec(MLA_KV_RANK),
                  _const_spec(wqa.shape), _const_spec(wqb.shape), _const_spec(wk.shape),
                  _const_spec(wv.shape), row(V7X_LANES), row(V7X_LANES)],
        out_specs=[row(SGU_WIDTH), row(hq), row(hq), row(hq)],
        out_shape=[jax.ShapeDtypeStruct((s, SGU_WIDTH), BF16),
                   jax.ShapeDtypeStruct((s, hq), BF16),
                   jax.ShapeDtypeStruct((s, hq), BF16),
                   jax.ShapeDtypeStruct((s, hq), BF16)],
        scratch_shapes=[pltpu.VMEM((tm, nproj), F32), pltpu.VMEM((tm, SGU_WIDTH), BF16)],
        compiler_params=_cparams(("parallel",), 56),
        name="hyb_in",
    )(x2, g, sc, sh, win, lng, sw, sb, qg, kvg, wqa, wqb, wk, wv, cos, sin)


def _flash_kernel(q_ref, k_ref, v_ref, o_ref, m_scr, acc_scr, *, tk):
    nk = k_ref.shape[0] // tk
    m_scr[...] = jnp.full(m_scr.shape, -jnp.inf, F32)
    acc_scr[...] = jnp.zeros(acc_scr.shape, F32)
    q = q_ref[...]

    def body(j, carry):
        off = pl.multiple_of(j * tk, tk)
        k = k_ref[pl.ds(off, tk), :]
        v = v_ref[pl.ds(off, tk), :]
        s = lax.dot_general(q, k, (((1,), (1,)), ((), ())), preferred_element_type=F32)
        m_prev = m_scr[...]
        m_new = jnp.maximum(m_prev, jnp.max(s, axis=-1, keepdims=True))
        alpha = jnp.exp2(m_prev - m_new)
        p = jnp.exp2(s - pltpu.repeat(m_new, tk // V7X_LANES, axis=1))
        acc_scr[...] = (acc_scr[...] * pltpu.repeat(alpha, 2, axis=1)
                        + jnp.dot(p.astype(BF16), v, preferred_element_type=F32))
        m_scr[...] = m_new
        return carry

    lax.fori_loop(0, nk, body, 0)
    acc = acc_scr[...]
    o_ref[...] = (acc[:, :MLA_V] / acc[:, MLA_V:]).astype(BF16)


def _flash(q, k, v):
    s = q.shape[0]
    tq = min(s, 1024)
    tk = min(s, 512)
    return pl.pallas_call(
        functools.partial(_flash_kernel, tk=tk),
        grid=(MLA_HEADS, s // tq),
        in_specs=[pl.BlockSpec((tq, QK_PAD), lambda h, i: (i, h)),
                  pl.BlockSpec((s, QK_PAD), lambda h, i: (0, h)),
                  pl.BlockSpec((s, QK_PAD), lambda h, i: (0, h))],
        out_specs=pl.BlockSpec((tq, MLA_V), lambda h, i: (i, h)),
        out_shape=jax.ShapeDtypeStruct((s, MLA_HEADS * MLA_V), BF16),
        scratch_shapes=[pltpu.VMEM((tq, V7X_LANES), F32), pltpu.VMEM((tq, QK_PAD), F32)],
        compiler_params=_cparams(("parallel", "arbitrary"), 56),
        name="flash_attn",
    )(q, k, v)


def _hyb_out_kernel(x_ref, a_ref, o_ref, w_ref, gate_ref, y_ref):
    m = (jnp.dot(a_ref[...], w_ref[:SGU_WIDTH, :], preferred_element_type=F32)
         + jnp.dot(o_ref[...], w_ref[SGU_WIDTH:, :], preferred_element_type=F32))
    y_ref[...] = x_ref[...] + gate_ref[...] * m


def _hyb_out(x2, a, o, w, gate):
    s, d = x2.shape
    tm = min(s, 512)
    return pl.pallas_call(
        _hyb_out_kernel,
        grid=(s // tm,),
        in_specs=[pl.BlockSpec((tm, d), lambda i: (i, 0)),
                  pl.BlockSpec((tm, a.shape[1]), lambda i: (i, 0)),
                  pl.BlockSpec((tm, o.shape[1]), lambda i: (i, 0)),
                  _const_spec(w.shape),
                  pl.BlockSpec((1, d), lambda i: (0, 0))],
        out_specs=pl.BlockSpec((tm, d), lambda i: (i, 0)),
        out_shape=jax.ShapeDtypeStruct((s, d), F32),
        compiler_params=_cparams(("parallel",), 48),
        name="hyb_out",
    )(x2, a, o, w, gate)


def _ffn_kernel(x_ref, g_ref, sc_ref, sh_ref, gate_ref, fg_ref, w1_ref, w2_ref, o_ref,
                h_scr, acc_scr, *, final_norm):
    k = pl.program_id(1)

    @pl.when(k == 0)
    def _():
        h_scr[...] = _norm_mod(x_ref[...], g_ref[...], sc_ref[...], sh_ref[...]).astype(BF16)
        acc_scr[...] = jnp.zeros(acc_scr.shape, F32)

    hid = jnp.dot(h_scr[...], w1_ref[...], preferred_element_type=F32)
    hid = jnp.square(jnp.maximum(hid, 0.0)).astype(BF16)
    acc_scr[...] += jnp.dot(hid, w2_ref[...], preferred_element_type=F32)

    @pl.when(k == pl.num_programs(1) - 1)
    def _():
        y = x_ref[...] + gate_ref[...] * acc_scr[...]
        if final_norm:
            y = y * lax.rsqrt(jnp.mean(y * y, axis=-1, keepdims=True) + EPS) * fg_ref[...]
        o_ref[...] = y


def _ffn(x2, g, sc, sh, gate, fg, w1, w2, final_norm):
    s, d = x2.shape
    hidden = w1.shape[1]
    tm = min(s, 512)
    tk = 1024
    vec = pl.BlockSpec((1, d), lambda i, k: (0, 0))
    return pl.pallas_call(
        functools.partial(_ffn_kernel, final_norm=final_norm),
        grid=(s // tm, hidden // tk),
        in_specs=[pl.BlockSpec((tm, d), lambda i, k: (i, 0)), vec, vec, vec, vec, vec,
                  pl.BlockSpec((d, tk), lambda i, k: (0, k)),
                  pl.BlockSpec((tk, d), lambda i, k: (k, 0))],
        out_specs=pl.BlockSpec((tm, d), lambda i, k: (i, 0)),
        out_shape=jax.ShapeDtypeStruct((s, d), F32),
        scratch_shapes=[pltpu.VMEM((tm, d), BF16), pltpu.VMEM((tm, d), F32)],
        compiler_params=_cparams(("parallel", "arbitrary"), 48),
        name="ffn",
    )(x2, g, sc, sh, gate, fg, w1, w2)


def _ssm_in_kernel(x_ref, g_ref, sc_ref, sh_ref, w_ref, wdt_ref, dtb_ref, zx_ref, dt_ref, h_scr):
    @pl.when(pl.program_id(1) == 0)
    def _():
        hb = _norm_mod(x_ref[...], g_ref[...], sc_ref[...], sh_ref[...]).astype(BF16)
        h_scr[...] = hb
        r = jnp.dot(hb, wdt_ref[...], preferred_element_type=F32) + dtb_ref[...]
        dt_ref[...] = jnp.maximum(r, 0.0) + jnp.log1p(jnp.exp(-jnp.abs(r)))

    zx_ref[...] = jnp.dot(h_scr[...], w_ref[...], preferred_element_type=F32).astype(BF16)


def _ssm_in(x2, g, sc, sh, w_zx, w_dt, dt_bias):
    s, d = x2.shape
    n = w_zx.shape[1]
    ndt = w_dt.shape[1]
    tm = min(s, 1024)
    tn = 1024
    vec = pl.BlockSpec((1, d), lambda i, j: (0, 0))
    return pl.pallas_call(
        _ssm_in_kernel,
        grid=(s // tm, n // tn),
        in_specs=[pl.BlockSpec((tm, d), lambda i, j: (i, 0)), vec, vec, vec,
                  pl.BlockSpec((d, tn), lambda i, j: (0, j)),
                  pl.BlockSpec((d, ndt), lambda i, j: (0, 0)),
                  pl.BlockSpec((1, ndt), lambda i, j: (0, 0))],
        out_specs=[pl.BlockSpec((tm, tn), lambda i, j: (i, j)),
                   pl.BlockSpec((tm, ndt), lambda i, j: (i, 0))],
        out_shape=[jax.ShapeDtypeStruct((s, n), BF16), jax.ShapeDtypeStruct((s, ndt), F32)],
        scratch_shapes=[pltpu.VMEM((tm, d), BF16)],
        compiler_params=_cparams(("parallel", "arbitrary"), 48),
        name="ssm_in",
    )(x2, g, sc, sh, w_zx, w_dt, dt_bias)


def _conv_kernel(prev_ref, main_ref, next_ref, w_ref, b_ref, o_ref, ext_scr):
    i = pl.program_id(0)
    tm = main_ref.shape[0]
    halo = prev_ref.shape[0]
    pad = SSM_CONV // 2
    ext_scr[0:halo, :] = jnp.where(i == 0, 0.0, prev_ref[...].astype(F32))
    ext_scr[halo:halo + tm, :] = main_ref[...].astype(F32)
    ext_scr[halo + tm:, :] = jnp.where(i == pl.num_programs(0) - 1, 0.0, next_ref[...].astype(F32))
    acc = jnp.broadcast_to(b_ref[...], (tm, b_ref.shape[1]))
    for t in range(SSM_CONV):
        acc = acc + w_ref[t:t + 1, :] * ext_scr[pl.ds(halo - pad + t, tm), :]
    y = _silu(acc)
    for j in range(o_ref.shape[0]):
        o_ref[j] = y[:, j * V7X_LANES:(j + 1) * V7X_LANES].astype(BF16)


def _conv_silu(zx, conv_w, conv_b):
    s = zx.shape[0]
    tm = min(s, 512)
    cw = 1024
    halo = V7X_BF16_SUBLANE_TILE
    col0 = SSM_INNER // cw
    nrow_h = s // halo
    return pl.pallas_call(
        _conv_kernel,
        grid=(s // tm, SSM_CONV_CH // cw),
        in_specs=[pl.BlockSpec((halo, cw), lambda i, j: (jnp.maximum(i * (tm // halo) - 1, 0), col0 + j)),
                  pl.BlockSpec((tm, cw), lambda i, j: (i, col0 + j)),
                  pl.BlockSpec((halo, cw), lambda i, j: (jnp.minimum((i + 1) * (tm // halo), nrow_h - 1), col0 + j)),
                  pl.BlockSpec((SSM_CONV, cw), lambda i, j: (0, j)),
                  pl.BlockSpec((1, cw), lambda i, j: (0, j))],
        out_specs=pl.BlockSpec((cw // V7X_LANES, tm, V7X_LANES), lambda i, j: (j, i, 0)),
        out_shape=jax.ShapeDtypeStruct((XBC_BLOCKS, s, V7X_LANES), BF16),
        scratch_shapes=[pltpu.VMEM((tm + 2 * halo, cw), F32)],
        compiler_params=_cparams(("parallel", "parallel"), 40),
        name="conv_silu",
    )(zx, zx, zx, conv_w, conv_b)


def _ssd_kernel(x_ref, b_ref, c_ref, dtt_ref, alog_ref, y_ref, st_scr, *, reverse):
    L = x_ref.shape[1]
    P = SSM_HEAD_DIM

    @pl.when(pl.program_id(1) == 0)
    def _():
        st_scr[...] = jnp.zeros(st_scr.shape, F32)

    dtt = dtt_ref[...]
    dta = dtt * (-jnp.exp(alog_ref[...]))
    kk = lax.broadcasted_iota(jnp.int32, (L, L), 0)
    ii = lax.broadcasted_iota(jnp.int32, (L, L), 1)
    tri = jnp.where((kk >= ii) if reverse else (kk <= ii), 1.0, 0.0).astype(BF16)
    hi = dta.astype(BF16)
    r1 = dta - hi.astype(F32)
    mid = r1.astype(BF16)
    lo = (r1 - mid.astype(F32)).astype(BF16)
    parts = jnp.concatenate([hi, mid, lo, jnp.zeros_like(hi)], axis=0)
    cs = jnp.dot(parts, tri, preferred_element_type=F32)
    cum = cs[0:8] + cs[8:16] + cs[16:24]
    end = 0 if reverse else L - 1
    cum_end = cum[:, end:end + 1]
    cum_dt = cum - jnp.log(dtt)
    to_end = jnp.exp(cum_end - cum_dt)
    dec_end = jnp.exp(cum_end)

    cg = c_ref[0]
    bt = b_ref[0].astype(F32).T
    cb = jnp.dot(cg, bt.astype(BF16), preferred_element_type=F32)
    st = st_scr[...]
    yst = jnp.dot(cg, st.astype(BF16), preferred_element_type=F32)
    keep = (kk <= ii) if reverse else (kk >= ii)
    low = lax.broadcasted_iota(jnp.int32, (1, V7X_LANES), 1) < P

    for q in range(x_ref.shape[0]):
        xp = x_ref[q]
        lanes = slice(q * V7X_LANES, (q + 1) * V7X_LANES)
        ys = []
        sus = []
        for e in range(V7X_LANES // P):
            hd = q * (V7X_LANES // P) + e
            cum_col = jnp.broadcast_to(cum[hd:hd + 1, :], (V7X_LANES, L)).T
            seg = pltpu.repeat(cum_col, L // V7X_LANES, axis=1) - cum_dt[hd:hd + 1, :]
            w = jnp.exp(jnp.where(keep, seg, -jnp.inf)) * cb
            yh = jnp.dot(w.astype(BF16), xp, preferred_element_type=F32)
            ys.append(yh + yst[:, lanes] * jnp.exp(cum_col))
            bts = (bt * to_end[hd:hd + 1, :]).astype(BF16)
            sus.append(jnp.dot(bts, xp, preferred_element_type=F32))
        y_ref[q] = jnp.where(low, ys[0], ys[1]).astype(BF16)
        dec = jnp.where(low, dec_end[2 * q:2 * q + 1, :], dec_end[2 * q + 1:2 * q + 2, :])
        st_scr[:, lanes] = st[:, lanes] * dec + jnp.where(low, sus[0], sus[1])


def _ssd_scan(xbc_t, dtt, alog_col, direction):
    s = xbc_t.shape[1]
    L = SSM_CHUNK
    nc = s // L
    reverse = direction == 1
    cidx = (lambda c: nc - 1 - c) if reverse else (lambda c: c)
    hrow = direction * SSM_GROUPS
    return pl.pallas_call(
        functools.partial(_ssd_kernel, reverse=reverse),
        grid=(SSM_GROUPS, nc),
        in_specs=[pl.BlockSpec((PAIRS_PER_GROUP, L, V7X_LANES), lambda g, c: (g, cidx(c), 0)),
                  pl.BlockSpec((1, L, V7X_LANES), lambda g, c: (X_BLOCKS + g, cidx(c), 0)),
                  pl.BlockSpec((1, L, V7X_LANES), lambda g, c: (X_BLOCKS + SSM_GROUPS + g, cidx(c), 0)),
                  pl.BlockSpec((HEADS_PER_GROUP, L), lambda g, c: (hrow + g, cidx(c))),
                  pl.BlockSpec((HEADS_PER_GROUP, 1), lambda g, c: (hrow + g, 0))],
        out_specs=pl.BlockSpec((PAIRS_PER_GROUP, L, V7X_LANES), lambda g, c: (g, cidx(c), 0)),
        out_shape=jax.ShapeDtypeStruct((X_BLOCKS, s, V7X_LANES), BF16),
        scratch_shapes=[pltpu.VMEM((SSM_STATE, HEADS_PER_GROUP * SSM_HEAD_DIM), F32)],
        compiler_params=_cparams(("parallel", "arbitrary"), 32),
        name="ssd_bwd" if reverse else "ssd_fwd",
    )(xbc_t, xbc_t, xbc_t, dtt, alog_col)


def _ssm_out_kernel(x_ref, yf_ref, yb_ref, xs_ref, z_ref, d_ref, ng_ref, w_ref, gate_ref, o_ref):
    per_group = PAIRS_PER_GROUP
    gw = SSM_INNER // SSM_GROUPS
    pieces = []
    for g in range(SSM_GROUPS):
        ys = []
        ss = None
        for p in range(per_group):
            j = g * per_group + p
            lanes = slice(j * V7X_LANES, (j + 1) * V7X_LANES)
            y = (yf_ref[j].astype(F32) + yb_ref[j].astype(F32)
                 + d_ref[:, lanes] * xs_ref[j].astype(F32))
            y = y * _silu(z_ref[:, lanes].astype(F32))
            ys.append(y)
            t = jnp.sum(y * y, axis=-1, keepdims=True)
            ss = t if ss is None else ss + t
        scale = lax.rsqrt(ss * (1.0 / gw) + EPS)
        for p in range(per_group):
            j = g * per_group + p
            lanes = slice(j * V7X_LANES, (j + 1) * V7X_LANES)
            pieces.append((ys[p] * scale * ng_ref[:, lanes]).astype(BF16))
    yn = jnp.concatenate(pieces, axis=1)
    o_ref[...] = x_ref[...] + gate_ref[...] * jnp.dot(yn, w_ref[...], preferred_element_type=F32)


def _ssm_out(x2, yf, yb, xbc_t, zx, d_full, ng, w, gate):
    s, d = x2.shape
    tm = min(s, 256)
    yspec = pl.BlockSpec((X_BLOCKS, tm, V7X_LANES), lambda i: (0, i, 0))
    return pl.pallas_call(
        _ssm_out_kernel,
        grid=(s // tm,),
        in_specs=[pl.BlockSpec((tm, d), lambda i: (i, 0)), yspec, yspec, yspec,
                  pl.BlockSpec((tm, SSM_INNER), lambda i: (i, 0)),
                  pl.BlockSpec((1, SSM_INNER), lambda i: (0, 0)),
                  pl.BlockSpec((1, SSM_INNER), lambda i: (0, 0)),
                  _const_spec(w.shape),
                  pl.BlockSpec((1, d), lambda i: (0, 0))],
        out_specs=pl.BlockSpec((tm, d), lambda i: (i, 0)),
        out_shape=jax.ShapeDtypeStruct((s, d), F32),
        compiler_params=_cparams(("parallel",), 56),
        name="ssm_out",
    )(x2, yf, yb, xbc_t, zx, d_full, ng, w, gate)


def _rot_cols(w):
    half = MLA_ROPE // 2
    return jnp.concatenate([-w[..., half:], w[..., :half]], axis=-1)


def _pad_lanes(w, width):
    return jnp.pad(w, [(0, 0)] * (w.ndim - 1) + [(0, width - w.shape[-1])])


def kernel(x, c, positions, ada_w, ada_b, norm_mix_g, norm_ffn_g, ffn_w1, ffn_w2, hyb_w_in, sgu_norm_g, sgu_w, sgu_b, mla_q_norm_g, mla_kv_norm_g, mla_w_uq, mla_w_ukv, hyb_w_out, ssm_w_in, ssm_conv_w, ssm_conv_b, ssm_dt_bias, ssm_a_log, ssm_d, ssm_norm_g, ssm_w_out, final_norm_g):
    batch, s, d = x.shape
    assert batch == 1 and d == D_MODEL and s % 1024 == 0
    depth = ada_w.shape[0]
    assert depth == 2
    x2 = x.reshape(s, d)

    mod = _ada_mod(c, ada_w, ada_b)
    mods = [[mod[l, :, i * d:(i + 1) * d] for i in range(6)] for l in range(depth)]
    row = lambda v: v.reshape(1, -1)

    sh1, sc1, g1, sh2, sc2, g2 = mods[0]
    w_in = hyb_w_in[0]
    c_pe = 2 * SGU_WIDTH + MLA_Q_RANK + MLA_KV_RANK
    w_kpe = w_in[:, c_pe:]
    win_ext = jnp.concatenate(
        [w_in[:, :c_pe], _pad_lanes(w_kpe, V7X_LANES), _pad_lanes(_rot_cols(w_kpe), V7X_LANES)],
        axis=1).astype(BF16)
    wq = mla_w_uq[0].reshape(MLA_Q_RANK, MLA_HEADS, MLA_NOPE + MLA_ROPE)
    wqa = _pad_lanes(wq, QK_PAD).reshape(MLA_Q_RANK, MLA_HEADS * QK_PAD).astype(BF16)
    wqb = _pad_lanes(_rot_cols(wq[..., MLA_NOPE:]), V7X_LANES).reshape(
        MLA_Q_RANK, MLA_HEADS * V7X_LANES).astype(BF16)
    wkv = mla_w_ukv[0].reshape(MLA_KV_RANK, MLA_HEADS, MLA_NOPE + MLA_V)
    wk = wkv[..., :MLA_NOPE].reshape(MLA_KV_RANK, MLA_HEADS * MLA_NOPE).astype(BF16)
    wv = wkv[..., MLA_NOPE:].reshape(MLA_KV_RANK, MLA_HEADS * MLA_V).astype(BF16)
    sb_full = jnp.repeat(sgu_b[0].T, SGU_HEAD, axis=1)

    half = MLA_ROPE // 2
    inv_freq = ROPE_THETA ** (-jnp.arange(half, dtype=F32) / half)
    inv128 = jnp.tile(inv_freq, V7X_LANES // half).reshape(1, V7X_LANES)
    cos, sin = _rope_tables(positions.astype(F32).reshape(s, 1), inv128)
    q_scale = (MLA_NOPE + MLA_ROPE) ** -0.5 * LOG2E

    a_out, q, k, v = _hyb_in(x2, row(norm_mix_g[0]), sc1, sh1, win_ext, row(sgu_norm_g[0]),
                             sgu_w[0].astype(BF16), sb_full, row(mla_q_norm_g[0]),
                             row(mla_kv_norm_g[0]), wqa, wqb, wk, wv, cos, sin, q_scale)
    o = _flash(q, k, v)
    x2 = _hyb_out(x2, a_out, o, hyb_w_out[0].astype(BF16), g1)
    x2 = _ffn(x2, row(norm_ffn_g[0]), sc2, sh2, g2, row(final_norm_g),
              ffn_w1[0].astype(BF16), ffn_w2[0].astype(BF16), False)

    sh1, sc1, g1, sh2, sc2, g2 = mods[1]
    w_in = ssm_w_in[0]
    zx, dt = _ssm_in(x2, row(norm_mix_g[1]), sc1, sh1, w_in[:, :ZX_WIDTH].astype(BF16),
                     w_in[:, ZX_WIDTH:].astype(BF16), ssm_dt_bias[0].reshape(1, 2 * SSM_HEADS))
    xbc_t = _conv_silu(zx, ssm_conv_w[0], row(ssm_conv_b[0]))
    dtt = dt.T
    alog_col = ssm_a_log[0].reshape(2 * SSM_HEADS, 1)
    yf = _ssd_scan(xbc_t, dtt, alog_col, 0)
    yb = _ssd_scan(xbc_t, dtt, alog_col, 1)
    d_full = jnp.repeat(ssm_d[0], SSM_HEAD_DIM).reshape(1, SSM_INNER)
    x2 = _ssm_out(x2, yf, yb, xbc_t, zx, d_full, row(ssm_norm_g[0]), ssm_w_out[0].astype(BF16), g1)
    x2 = _ffn(x2, row(norm_ffn_g[1]), sc2, sh2, g2, row(final_norm_g),
              ffn_w1[1].astype(BF16), ffn_w2[1].astype(BF16), True)
    return x2.reshape(batch, s, d)
```

```python
import functools
import math

import jax
import jax.numpy as jnp
from jax import lax
from jax.experimental import pallas as pl
from jax.experimental.pallas import tpu as pltpu

F32 = jnp.float32
BF16 = jnp.bfloat16

V7X_LANES = 128
V7X_BF16_SUBLANE_TILE = 16
V7X_VMEM_BYTES = 64 * 1024 * 1024

D_MODEL = 2048
SGU_CHUNK = 128
SGU_GROUPS = 8
SGU_HEAD = 128
SGU_WIDTH = SGU_GROUPS * SGU_HEAD
MLA_HEADS = 8
MLA_Q_RANK = 512
MLA_KV_RANK = 512
MLA_NOPE = 128
MLA_ROPE = 64
MLA_V = 128
ROPE_THETA = 10000.0
HYB_MIX = SGU_WIDTH + MLA_HEADS * MLA_V
SSM_INNER = 2 * D_MODEL
SSM_HEAD_DIM = 64
SSM_HEADS = SSM_INNER // SSM_HEAD_DIM
SSM_GROUPS = 8
SSM_STATE = 128
SSM_CONV = 5
SSM_CHUNK = 256
SSM_CONV_CH = SSM_INNER + 2 * SSM_GROUPS * SSM_STATE
FFN_HIDDEN = 4 * D_MODEL
EPS = 1e-6

QK_PAD = 2 * V7X_LANES
HEADS_PER_GROUP = SSM_HEADS // SSM_GROUPS
PAIRS_PER_GROUP = HEADS_PER_GROUP * SSM_HEAD_DIM // V7X_LANES
X_BLOCKS = SSM_INNER // V7X_LANES
XBC_BLOCKS = SSM_CONV_CH // V7X_LANES
ZX_WIDTH = SSM_INNER + SSM_CONV_CH
LOG2E = 1.4426950408889634


def _cparams(sem, vmem_mib):
    return pltpu.CompilerParams(dimension_semantics=sem,
                                vmem_limit_bytes=vmem_mib * 1024 * 1024)


def _const_spec(shape):
    nd = len(shape)
    return pl.BlockSpec(shape, lambda *_: (0,) * nd, pipeline_mode=pl.Buffered(1))


def _norm_mod(x, g, sc, sh):
    y = x * lax.rsqrt(jnp.mean(x * x, axis=-1, keepdims=True) + EPS)
    return (y * g) * (1.0 + sc) + sh


def _silu(x):
    return x * jax.nn.sigmoid(x)


def _ada_kernel(c_ref, w_ref, b_ref, o_ref):
    c = c_ref[...]
    cond = jnp.broadcast_to(_silu(c), (8, c.shape[-1]))
    r = jnp.dot(cond, w_ref[0], preferred_element_type=F32,
                precision=lax.Precision.HIGHEST)
    o_ref[0] = r[0:1] + b_ref[0]


def _ada_mod(c, ada_w, ada_b):
    depth, d, n = ada_w.shape
    tn = 1024
    return pl.pallas_call(
        _ada_kernel,
        grid=(depth, n // tn),
        in_specs=[pl.BlockSpec((1, d), lambda l, j: (0, 0)),
                  pl.BlockSpec((1, d, tn), lambda l, j: (l, 0, j)),
                  pl.BlockSpec((1, 1, tn), lambda l, j: (l, 0, j))],
        out_specs=pl.BlockSpec((1, 1, tn), lambda l, j: (l, 0, j)),
        out_shape=jax.ShapeDtypeStruct((depth, 1, n), F32),
        compiler_params=_cparams(("parallel", "parallel"), 40),
        name="ada_mod",
    )(c, ada_w, ada_b.reshape(depth, 1, n))


def _rope_kernel(pos_ref, inv_ref, cos_ref, sin_ref):
    ang = pos_ref[...] * inv_ref[...]
    cos_ref[...] = jnp.cos(ang)
    sin_ref[...] = jnp.sin(ang)


def _rope_tables(pos_col, inv128):
    s = pos_col.shape[0]
    tm = min(s, 2048)
    return pl.pallas_call(
        _rope_kernel,
        grid=(s // tm,),
        in_specs=[pl.BlockSpec((tm, 1), lambda i: (i, 0)),
                  pl.BlockSpec((1, V7X_LANES), lambda i: (0, 0))],
        out_specs=[pl.BlockSpec((tm, V7X_LANES), lambda i: (i, 0))] * 2,
        out_shape=[jax.ShapeDtypeStruct((s, V7X_LANES), F32)] * 2,
        compiler_params=_cparams(("parallel",), 32),
        name="rope_tables",
    )(pos_col, inv128)


def _hyb_in_kernel(x_ref, g_ref, sc_ref, sh_ref, win_ref, lng_ref, sw_ref, sb_ref,
                   qg_ref, kvg_ref, wqa_ref, wqb_ref, wk_ref, wv_ref, cos_ref, sin_ref,
                   a_ref, q_ref, k_ref, v_ref, proj_scr, vn_scr, *, q_scale):
    tm = x_ref.shape[0]
    h = _norm_mod(x_ref[...], g_ref[...], sc_ref[...], sh_ref[...])
    proj_scr[...] = jnp.dot(h.astype(BF16), win_ref[...], preferred_element_type=F32)

    v = jax.nn.gelu(proj_scr[:, SGU_WIDTH:2 * SGU_WIDTH])
    vc = v - jnp.mean(v, axis=-1, keepdims=True)
    vn = vc * lax.rsqrt(jnp.mean(vc * vc, axis=-1, keepdims=True) + EPS) * lng_ref[...]
    vn_scr[...] = vn.astype(BF16)
    for c in range(tm // SGU_CHUNK):
        rows = slice(c * SGU_CHUNK, (c + 1) * SGU_CHUNK)
        for g in range(SGU_GROUPS):
            cols = slice(g * SGU_HEAD, (g + 1) * SGU_HEAD)
            mixed = jnp.dot(sw_ref[g], vn_scr[rows, cols], preferred_element_type=F32)
            u = jax.nn.gelu(proj_scr[rows, cols])
            a_ref[rows, cols] = (u * (mixed + sb_ref[:, cols])).astype(BF16)

    cos = cos_ref[...]
    sin = sin_ref[...]
    lat0 = 2 * SGU_WIDTH
    ql = proj_scr[:, lat0:lat0 + MLA_Q_RANK]
    qn = (ql * lax.rsqrt(jnp.mean(ql * ql, axis=-1, keepdims=True) + EPS) * qg_ref[...]).astype(BF16)
    qa = jnp.dot(qn, wqa_ref[...], preferred_element_type=F32)
    qb = jnp.dot(qn, wqb_ref[...], preferred_element_type=F32)
    for hd in range(MLA_HEADS):
        o = hd * QK_PAD
        q_ref[:, o:o + V7X_LANES] = (qa[:, o:o + V7X_LANES] * q_scale).astype(BF16)
        pe = (qa[:, o + V7X_LANES:o + QK_PAD] * cos
              + qb[:, hd * V7X_LANES:(hd + 1) * V7X_LANES] * sin)
        q_ref[:, o + V7X_LANES:o + QK_PAD] = (pe * q_scale).astype(BF16)

    kv0 = lat0 + MLA_Q_RANK
    kvl = proj_scr[:, kv0:kv0 + MLA_KV_RANK]
    kvn = (kvl * lax.rsqrt(jnp.mean(kvl * kvl, axis=-1, keepdims=True) + EPS) * kvg_ref[...]).astype(BF16)
    kn = jnp.dot(kvn, wk_ref[...], preferred_element_type=F32)
    vv = jnp.dot(kvn, wv_ref[...], preferred_element_type=F32)
    pe0 = kv0 + MLA_KV_RANK
    kpe = (proj_scr[:, pe0:pe0 + V7X_LANES] * cos
           + proj_scr[:, pe0 + V7X_LANES:pe0 + 2 * V7X_LANES] * sin).astype(BF16)
    ones = jnp.ones((tm, V7X_LANES), BF16)
    for hd in range(MLA_HEADS):
        o = hd * QK_PAD
        hs = slice(hd * V7X_LANES, (hd + 1) * V7X_LANES)
        k_ref[:, o:o + V7X_LANES] = kn[:, hs].astype(BF16)
        k_ref[:, o + V7X_LANES:o + QK_PAD] = kpe
        v_ref[:, o:o + V7X_LANES] = vv[:, hs].astype(BF16)
        v_ref[:, o + V7X_LANES:o + QK_PAD] = ones


def _hyb_in(x2, g, sc, sh, win, lng, sw, sb, qg, kvg, wqa, wqb, wk, wv, cos, sin, q_scale):
    s, d = x2.shape
    tm = min(s, 512)
    nproj = win.shape[1]
    row = lambda w: pl.BlockSpec((tm, w), lambda i: (i, 0))
    vec = lambda w: pl.BlockSpec((1, w), lambda i: (0, 0))
    hq = MLA_HEADS * QK_PAD
    return pl.pallas_call(
        functools.partial(_hyb_in_kernel, q_scale=q_scale),
        grid=(s // tm,),
        in_specs=[row(d), vec(d), vec(d), vec(d), _const_spec(win.shape), vec(SGU_WIDTH),
                  _const_spec(sw.shape), _const_spec(sb.shape), vec(MLA_Q_RANK), vec(MLA_KV_RANK),
                  _const_spec(wqa.shape), _const_spec(wqb.shape), _const_spec(wk.shape),
                  _const_spec(wv.shape), row(V7X_LANES), row(V7X_LANES)],
        out_specs=[row(SGU_WIDTH), row(hq), row(hq), row(hq)],
        out_shape=[jax.ShapeDtypeStruct((s, SGU_WIDTH), BF16),
                   jax.ShapeDtypeStruct((s, hq), BF16),
                   jax.ShapeDtypeStruct((s, hq), BF16),
                   jax.ShapeDtypeStruct((s, hq), BF16)],
        scratch_shapes=[pltpu.VMEM((tm, nproj), F32), pltpu.VMEM((tm, SGU_WIDTH), BF16)],
        compiler_params=_cparams(("parallel",), 56),
        name="hyb_in",
    )(x2, g, sc, sh, win, lng, sw, sb, qg, kvg, wqa, wqb, wk, wv, cos, sin)


def _flash_kernel(q_ref, k_ref, v_ref, o_ref, m_scr, acc_scr, *, tk, sub):
    nk = k_ref.shape[0] // tk
    tq = q_ref.shape[0]
    m_scr[...] = jnp.full(m_scr.shape, -jnp.inf, F32)
    acc_scr[...] = jnp.zeros(acc_scr.shape, F32)

    def body(j, carry):
        off = pl.multiple_of(j * tk, tk)
        k = k_ref[pl.ds(off, tk), :]
        v = v_ref[pl.ds(off, tk), :]
        for r in range(tq // sub):
            rows = slice(r * sub, (r + 1) * sub)
            s = lax.dot_general(q_ref[rows, :], k, (((1,), (1,)), ((), ())),
                                preferred_element_type=F32)
            m_prev = m_scr[rows, :]
            m_new = jnp.maximum(m_prev, jnp.max(s, axis=-1, keepdims=True))
            alpha = jnp.exp2(m_prev - m_new)
            p = jnp.exp2(s - pltpu.repeat(m_new, tk // V7X_LANES, axis=1))
            acc_scr[rows, :] = (acc_scr[rows, :] * pltpu.repeat(alpha, 2, axis=1)
                                + jnp.dot(p.astype(BF16), v, preferred_element_type=F32))
            m_scr[rows, :] = m_new
        return carry

    lax.fori_loop(0, nk, body, 0)
    acc = acc_scr[...]
    o_ref[...] = (acc[:, :MLA_V] / acc[:, MLA_V:]).astype(BF16)


def _flash(q, k, v):
    s = q.shape[0]
    tq = min(s, 2048)
    tk = min(s, 1024)
    return pl.pallas_call(
        functools.partial(_flash_kernel, tk=tk, sub=min(tq, 256)),
        grid=(MLA_HEADS, s // tq),
        in_specs=[pl.BlockSpec((tq, QK_PAD), lambda h, i: (i, h)),
                  pl.BlockSpec((s, QK_PAD), lambda h, i: (0, h)),
                  pl.BlockSpec((s, QK_PAD), lambda h, i: (0, h))],
        out_specs=pl.BlockSpec((tq, MLA_V), lambda h, i: (i, h)),
        out_shape=jax.ShapeDtypeStruct((s, MLA_HEADS * MLA_V), BF16),
        scratch_shapes=[pltpu.VMEM((tq, V7X_LANES), F32), pltpu.VMEM((tq, QK_PAD), F32)],
        compiler_params=_cparams(("parallel", "arbitrary"), 56),
        name="flash_attn",
    )(q, k, v)


def _hyb_out_kernel(x_ref, a_ref, o_ref, w_ref, gate_ref, y_ref):
    m = (jnp.dot(a_ref[...], w_ref[:SGU_WIDTH, :], preferred_element_type=F32)
         + jnp.dot(o_ref[...], w_ref[SGU_WIDTH:, :], preferred_element_type=F32))
    y_ref[...] = x_ref[...] + gate_ref[...] * m


def _hyb_out(x2, a, o, w, gate):
    s, d = x2.shape
    tm = min(s, 512)
    return pl.pallas_call(
        _hyb_out_kernel,
        grid=(s // tm,),
        in_specs=[pl.BlockSpec((tm, d), lambda i: (i, 0)),
                  pl.BlockSpec((tm, a.shape[1]), lambda i: (i, 0)),
                  pl.BlockSpec((tm, o.shape[1]), lambda i: (i, 0)),
                  _const_spec(w.shape),
                  pl.BlockSpec((1, d), lambda i: (0, 0))],
        out_specs=pl.BlockSpec((tm, d), lambda i: (i, 0)),
        out_shape=jax.ShapeDtypeStruct((s, d), F32),
        compiler_params=_cparams(("parallel",), 48),
        name="hyb_out",
    )(x2, a, o, w, gate)


def _ffn_kernel(x_ref, g_ref, sc_ref, sh_ref, gate_ref, fg_ref, w1_ref, w2_ref, o_ref,
                h_scr, acc_scr, *, final_norm):
    k = pl.program_id(1)

    @pl.when(k == 0)
    def _():
        h_scr[...] = _norm_mod(x_ref[...], g_ref[...], sc_ref[...], sh_ref[...]).astype(BF16)
        acc_scr[...] = jnp.zeros(acc_scr.shape, F32)

    hid = jnp.dot(h_scr[...], w1_ref[...], preferred_element_type=F32)
    hid = jnp.square(jnp.maximum(hid, 0.0)).astype(BF16)
    acc_scr[...] += jnp.dot(hid, w2_ref[...], preferred_element_type=F32)

    @pl.when(k == pl.num_programs(1) - 1)
    def _():
        y = x_ref[...] + gate_ref[...] * acc_scr[...]
        if final_norm:
            y = y * lax.rsqrt(jnp.mean(y * y, axis=-1, keepdims=True) + EPS) * fg_ref[...]
        o_ref[...] = y


def _ffn(x2, g, sc, sh, gate, fg, w1, w2, final_norm):
    s, d = x2.shape
    hidden = w1.shape[1]
    tm = min(s, 512)
    tk = 1024
    vec = pl.BlockSpec((1, d), lambda i, k: (0, 0))
    return pl.pallas_call(
        functools.partial(_ffn_kernel, final_norm=final_norm),
        grid=(s // tm, hidden // tk),
        in_specs=[pl.BlockSpec((tm, d), lambda i, k: (i, 0)), vec, vec, vec, vec, vec,
                  pl.BlockSpec((d, tk), lambda i, k: (0, k)),
                  pl.BlockSpec((tk, d), lambda i, k: (k, 0))],
        out_specs=pl.BlockSpec((tm, d), lambda i, k: (i, 0)),
        out_shape=jax.ShapeDtypeStruct((s, d), F32),
        scratch_shapes=[pltpu.VMEM((tm, d), BF16), pltpu.VMEM((tm, d), F32)],
        compiler_params=_cparams(("parallel", "arbitrary"), 48),
        name="ffn",
    )(x2, g, sc, sh, gate, fg, w1, w2)


def _ssm_in_kernel(x_ref, g_ref, sc_ref, sh_ref, w_ref, wdt_ref, dtb_ref, zx_ref, dt_ref, h_scr):
    @pl.when(pl.program_id(1) == 0)
    def _():
        hb = _norm_mod(x_ref[...], g_ref[...], sc_ref[...], sh_ref[...]).astype(BF16)
        h_scr[...] = hb
        r = jnp.dot(hb, wdt_ref[...], preferred_element_type=F32) + dtb_ref[...]
        dt_ref[...] = jnp.maximum(r, 0.0) + jnp.log1p(jnp.exp(-jnp.abs(r)))

    zx_ref[...] = jnp.dot(h_scr[...], w_ref[...], preferred_element_type=F32).astype(BF16)


def _ssm_in(x2, g, sc, sh, w_zx, w_dt, dt_bias):
    s, d = x2.shape
    n = w_zx.shape[1]
    ndt = w_dt.shape[1]
    tm = min(s, 1024)
    tn = 1024
    vec = pl.BlockSpec((1, d), lambda i, j: (0, 0))
    return pl.pallas_call(
        _ssm_in_kernel,
        grid=(s // tm, n // tn),
        in_specs=[pl.BlockSpec((tm, d), lambda i, j: (i, 0)), vec, vec, vec,
                  pl.BlockSpec((d, tn), lambda i, j: (0, j)),
                  pl.BlockSpec((d, ndt), lambda i, j: (0, 0)),
                  pl.BlockSpec((1, ndt), lambda i, j: (0, 0))],
        out_specs=[pl.BlockSpec((tm, tn), lambda i, j: (i, j)),
                   pl.BlockSpec((tm, ndt), lambda i, j: (i, 0))],
        out_shape=[jax.ShapeDtypeStruct((s, n), BF16), jax.ShapeDtypeStruct((s, ndt), F32)],
        scratch_shapes=[pltpu.VMEM((tm, d), BF16)],
        compiler_params=_cparams(("parallel", "arbitrary"), 48),
        name="ssm_in",
    )(x2, g, sc, sh, w_zx, w_dt, dt_bias)


def _conv_kernel(prev_ref, main_ref, next_ref, w_ref, b_ref, o_ref, ext_scr):
    i = pl.program_id(0)
    tm = main_ref.shape[0]
    halo = prev_ref.shape[0]
    pad = SSM_CONV // 2
    ext_scr[0:halo, :] = jnp.where(i == 0, 0.0, prev_ref[...].astype(F32))
    ext_scr[halo:halo + tm, :] = main_ref[...].astype(F32)
    ext_scr[halo + tm:, :] = jnp.where(i == pl.num_programs(0) - 1, 0.0, next_ref[...].astype(F32))
    acc = jnp.broadcast_to(b_ref[...], (tm, b_ref.shape[1]))
    for t in range(SSM_CONV):
        acc = acc + w_ref[t:t + 1, :] * ext_scr[pl.ds(halo - pad + t, tm), :]
    y = _silu(acc)
    for j in range(o_ref.shape[0]):
        o_ref[j] = y[:, j * V7X_LANES:(j + 1) * V7X_LANES].astype(BF16)


def _conv_silu(zx, conv_w, conv_b):
    s = zx.shape[0]
    tm = min(s, 512)
    cw = 1024
    halo = V7X_BF16_SUBLANE_TILE
    col0 = SSM_INNER // cw
    nrow_h = s // halo
    return pl.pallas_call(
        _conv_kernel,
        grid=(s // tm, SSM_CONV_CH // cw),
        in_specs=[pl.BlockSpec((halo, cw), lambda i, j: (jnp.maximum(i * (tm // halo) - 1, 0), col0 + j)),
                  pl.BlockSpec((tm, cw), lambda i, j: (i, col0 + j)),
                  pl.BlockSpec((halo, cw), lambda i, j: (jnp.minimum((i + 1) * (tm // halo), nrow_h - 1), col0 + j)),
                  pl.BlockSpec((SSM_CONV, cw), lambda i, j: (0, j)),
                  pl.BlockSpec((1, cw), lambda i, j: (0, j))],
        out_specs=pl.BlockSpec((cw // V7X_LANES, tm, V7X_LANES), lambda i, j: (j, i, 0)),
        out_shape=jax.ShapeDtypeStruct((XBC_BLOCKS, s, V7X_LANES), BF16),
        scratch_shapes=[pltpu.VMEM((tm + 2 * halo, cw), F32)],
        compiler_params=_cparams(("parallel", "parallel"), 40),
        name="conv_silu",
    )(zx, zx, zx, conv_w, conv_b)


def _ssd_kernel(x_ref, b_ref, c_ref, dtt_ref, alog_ref, y_ref, st_scr, *, reverse):
    L = x_ref.shape[1]
    P = SSM_HEAD_DIM

    @pl.when(pl.program_id(1) == 0)
    def _():
        st_scr[...] = jnp.zeros(st_scr.shape, F32)

    dtt = dtt_ref[...]
    dta = dtt * (-jnp.exp(alog_ref[...]))
    kk = lax.broadcasted_iota(jnp.int32, (L, L), 0)
    ii = lax.broadcasted_iota(jnp.int32, (L, L), 1)
    tri = jnp.where((kk >= ii) if reverse else (kk <= ii), 1.0, 0.0).astype(BF16)
    hi = dta.astype(BF16)
    r1 = dta - hi.astype(F32)
    mid = r1.astype(BF16)
    lo = (r1 - mid.astype(F32)).astype(BF16)
    parts = jnp.concatenate([hi, mid, lo, jnp.zeros_like(hi)], axis=0)
    cs = jnp.dot(parts, tri, preferred_element_type=F32)
    cum = cs[0:8] + cs[8:16] + cs[16:24]
    end = 0 if reverse else L - 1
    cum_end = cum[:, end:end + 1]
    cum_dt = cum - jnp.log(dtt)
    to_end = jnp.exp(cum_end - cum_dt)
    dec_end = jnp.exp(cum_end)

    cg = c_ref[0]
    bt = b_ref[0].astype(F32).T
    cb = jnp.dot(cg, bt.astype(BF16), preferred_element_type=F32)
    st = st_scr[...]
    yst = jnp.dot(cg, st.astype(BF16), preferred_element_type=F32)
    keep = (kk <= ii) if reverse else (kk >= ii)
    low = lax.broadcasted_iota(jnp.int32, (1, V7X_LANES), 1) < P

    for q in range(x_ref.shape[0]):
        xp = x_ref[q]
        lanes = slice(q * V7X_LANES, (q + 1) * V7X_LANES)
        ys = []
        sus = []
        for e in range(V7X_LANES // P):
            hd = q * (V7X_LANES // P) + e
            cum_col = jnp.broadcast_to(cum[hd:hd + 1, :], (V7X_LANES, L)).T
            seg = pltpu.repeat(cum_col, L // V7X_LANES, axis=1) - cum_dt[hd:hd + 1, :]
            w = jnp.exp(jnp.where(keep, seg, -jnp.inf)) * cb
            yh = jnp.dot(w.astype(BF16), xp, preferred_element_type=F32)
            ys.append(yh + yst[:, lanes] * jnp.exp(cum_col))
            bts = (bt * to_end[hd:hd + 1, :]).astype(BF16)
            sus.append(jnp.dot(bts, xp, preferred_element_type=F32))
        y_ref[q] = jnp.where(low, ys[0], ys[1]).astype(BF16)
        dec = jnp.where(low, dec_end[2 * q:2 * q + 1, :], dec_end[2 * q + 1:2 * q + 2, :])
        st_scr[:, lanes] = st[:, lanes] * dec + jnp.where(low, sus[0], sus[1])


def _ssd_scan(xbc_t, dtt, alog_col, direction):
    s = xbc_t.shape[1]
    L = SSM_CHUNK
    nc = s // L
    reverse = direction == 1
    cidx = (lambda c: nc - 1 - c) if reverse else (lambda c: c)
    hrow = direction * SSM_GROUPS
    return pl.pallas_call(
        functools.partial(_ssd_kernel, reverse=reverse),
        grid=(SSM_GROUPS, nc),
        in_specs=[pl.BlockSpec((PAIRS_PER_GROUP, L, V7X_LANES), lambda g, c: (g, cidx(c), 0)),
                  pl.BlockSpec((1, L, V7X_LANES), lambda g, c: (X_BLOCKS + g, cidx(c), 0)),
                  pl.BlockSpec((1, L, V7X_LANES), lambda g, c: (X_BLOCKS + SSM_GROUPS + g, cidx(c), 0)),
                  pl.BlockSpec((HEADS_PER_GROUP, L), lambda g, c: (hrow + g, cidx(c))),
                  pl.BlockSpec((HEADS_PER_GROUP, 1), lambda g, c: (hrow + g, 0))],
        out_specs=pl.BlockSpec((PAIRS_PER_GROUP, L, V7X_LANES), lambda g, c: (g, cidx(c), 0)),
        out_shape=jax.ShapeDtypeStruct((X_BLOCKS, s, V7X_LANES), BF16),
        scratch_shapes=[pltpu.VMEM((SSM_STATE, HEADS_PER_GROUP * SSM_HEAD_DIM), F32)],
        compiler_params=_cparams(("parallel", "arbitrary"), 32),
        name="ssd_bwd" if reverse else "ssd_fwd",
    )(xbc_t, xbc_t, xbc_t, dtt, alog_col)


def _ssm_out_kernel(x_ref, yf_ref, yb_ref, xs_ref, z_ref, d_ref, ng_ref, w_ref, gate_ref, o_ref):
    per_group = PAIRS_PER_GROUP
    gw = SSM_INNER // SSM_GROUPS
    pieces = []
    for g in range(SSM_GROUPS):
        ys = []
        ss = None
        for p in range(per_group):
            j = g * per_group + p
            lanes = slice(j * V7X_LANES, (j + 1) * V7X_LANES)
            y = (yf_ref[j].astype(F32) + yb_ref[j].astype(F32)
                 + d_ref[:, lanes] * xs_ref[j].astype(F32))
            y = y * _silu(z_ref[:, lanes].astype(F32))
            ys.append(y)
            t = jnp.sum(y * y, axis=-1, keepdims=True)
            ss = t if ss is None else ss + t
        scale = lax.rsqrt(ss * (1.0 / gw) + EPS)
        for p in range(per_group):
            j = g * per_group + p
            lanes = slice(j * V7X_LANES, (j + 1) * V7X_LANES)
            pieces.append((ys[p] * scale * ng_ref[:, lanes]).astype(BF16))
    yn = jnp.concatenate(pieces, axis=1)
    o_ref[...] = x_ref[...] + gate_ref[...] * jnp.dot(yn, w_ref[...], preferred_element_type=F32)


def _ssm_out(x2, yf, yb, xbc_t, zx, d_full, ng, w, gate):
    s, d = x2.shape
    tm = min(s, 256)
    yspec = pl.BlockSpec((X_BLOCKS, tm, V7X_LANES), lambda i: (0, i, 0))
    return pl.pallas_call(
        _ssm_out_kernel,
        grid=(s // tm,),
        in_specs=[pl.BlockSpec((tm, d), lambda i: (i, 0)), yspec, yspec, yspec,
                  pl.BlockSpec((tm, SSM_INNER), lambda i: (i, 0)),
                  pl.BlockSpec((1, SSM_INNER), lambda i: (0, 0)),
                  pl.BlockSpec((1, SSM_INNER), lambda i: (0, 0)),
                  _const_spec(w.shape),
                  pl.BlockSpec((1, d), lambda i: (0, 0))],
        out_specs=pl.BlockSpec((tm, d), lambda i: (i, 0)),
        out_shape=jax.ShapeDtypeStruct((s, d), F32),
        compiler_params=_cparams(("parallel",), 56),
        name="ssm_out",
    )(x2, yf, yb, xbc_t, zx, d_full, ng, w, gate)


def _rot_cols(w):
    half = MLA_ROPE // 2
    return jnp.concatenate([-w[..., half:], w[..., :half]], axis=-1)


def _pad_lanes(w, width):
    return jnp.pad(w, [(0, 0)] * (w.ndim - 1) + [(0, width - w.shape[-1])])


def kernel(x, c, positions, ada_w, ada_b, norm_mix_g, norm_ffn_g, ffn_w1, ffn_w2, hyb_w_in, sgu_norm_g, sgu_w, sgu_b, mla_q_norm_g, mla_kv_norm_g, mla_w_uq, mla_w_ukv, hyb_w_out, ssm_w_in, ssm_conv_w, ssm_conv_b, ssm_dt_bias, ssm_a_log, ssm_d, ssm_norm_g, ssm_w_out, final_norm_g):
    batch, s, d = x.shape
    assert batch == 1 and d == D_MODEL and s % 1024 == 0
    depth = ada_w.shape[0]
    assert depth == 2
    x2 = x.reshape(s, d)

    mod = _ada_mod(c, ada_w, ada_b)
    mods = [[mod[l, :, i * d:(i + 1) * d] for i in range(6)] for l in range(depth)]
    row = lambda v: v.reshape(1, -1)

    sh1, sc1, g1, sh2, sc2, g2 = mods[0]
    w_in = hyb_w_in[0]
    c_pe = 2 * SGU_WIDTH + MLA_Q_RANK + MLA_KV_RANK
    w_kpe = w_in[:, c_pe:]
    win_ext = jnp.concatenate(
        [w_in[:, :c_pe], _pad_lanes(w_kpe, V7X_LANES), _pad_lanes(_rot_cols(w_kpe), V7X_LANES)],
        axis=1).astype(BF16)
    wq = mla_w_uq[0].reshape(MLA_Q_RANK, MLA_HEADS, MLA_NOPE + MLA_ROPE)
    wqa = _pad_lanes(wq, QK_PAD).reshape(MLA_Q_RANK, MLA_HEADS * QK_PAD).astype(BF16)
    wqb = _pad_lanes(_rot_cols(wq[..., MLA_NOPE:]), V7X_LANES).reshape(
        MLA_Q_RANK, MLA_HEADS * V7X_LANES).astype(BF16)
    wkv = mla_w_ukv[0].reshape(MLA_KV_RANK, MLA_HEADS, MLA_NOPE + MLA_V)
    wk = wkv[..., :MLA_NOPE].reshape(MLA_KV_RANK, MLA_HEADS * MLA_NOPE).astype(BF16)
    wv = wkv[..., MLA_NOPE:].reshape(MLA_KV_RANK, MLA_HEADS * MLA_V).astype(BF16)
    sb_full = jnp.repeat(sgu_b[0].T, SGU_HEAD, axis=1)

    half = MLA_ROPE // 2
    inv_freq = ROPE_THETA ** (-jnp.arange(half, dtype=F32) / half)
    inv128 = jnp.tile(inv_freq, V7X_LANES // half).reshape(1, V7X_LANES)
    cos, sin = _rope_tables(positions.astype(F32).reshape(s, 1), inv128)
    q_scale = (MLA_NOPE + MLA_ROPE) ** -0.5 * LOG2E

    a_out, q, k, v = _hyb_in(x2, row(norm_mix_g[0]), sc1, sh1, win_ext, row(sgu_norm_g[0]),
                             sgu_w[0].astype(BF16), sb_full, row(mla_q_norm_g[0]),
                             row(mla_kv_norm_g[0]), wqa, wqb, wk, wv, cos, sin, q_scale)
    o = _flash(q, k, v)
    x2 = _hyb_out(x2, a_out, o, hyb_w_out[0].astype(BF16), g1)
    x2 = _ffn(x2, row(norm_ffn_g[0]), sc2, sh2, g2, row(final_norm_g),
              ffn_w1[0].astype(BF16), ffn_w2[0].astype(BF16), False)

    sh1, sc1, g1, sh2, sc2, g2 = mods[1]
    w_in = ssm_w_in[0]
    zx, dt = _ssm_in(x2, row(norm_mix_g[1]), sc1, sh1, w_in[:, :ZX_WIDTH].astype(BF16),
                     w_in[:, ZX_WIDTH:].astype(BF16), ssm_dt_bias[0].reshape(1, 2 * SSM_HEADS))
    xbc_t = _conv_silu(zx, ssm_conv_w[0], row(ssm_conv_b[0]))
    dtt = dt.T
    alog_col = ssm_a_log[0].reshape(2 * SSM_HEADS, 1)
    yf = _ssd_scan(xbc_t, dtt, alog_col, 0)
    yb = _ssd_scan(xbc_t, dtt, alog_col, 1)
    d_full = jnp.repeat(ssm_d[0], SSM_HEAD_DIM).reshape(1, SSM_INNER)
    x2 = _ssm_out(x2, yf, yb, xbc_t, zx, d_full, row(ssm_norm_g[0]), ssm_w_out[0].astype(BF16), g1)
    x2 = _ffn(x2, row(norm_ffn_g[1]), sc2, sh2, g2, row(final_norm_g),
              ffn_w1[1].astype(BF16), ffn_w2[1].astype(BF16), True)
    return x2.reshape(batch, s, d)
```

```python
import functools
import math

import jax
import jax.numpy as jnp
from jax import lax
from jax.experimental import pallas as pl
from jax.experimental.pallas import tpu as pltpu

F32 = jnp.float32
BF16 = jnp.bfloat16

V7X_LANES = 128
V7X_BF16_SUBLANE_TILE = 16
V7X_VMEM_BYTES = 64 * 1024 * 1024

D_MODEL = 2048
SGU_CHUNK = 128
SGU_GROUPS = 8
SGU_HEAD = 128
SGU_WIDTH = SGU_GROUPS * SGU_HEAD
MLA_HEADS = 8
MLA_Q_RANK = 512
MLA_KV_RANK = 512
MLA_NOPE = 128
MLA_ROPE = 64
MLA_V = 128
ROPE_THETA = 10000.0
HYB_MIX = SGU_WIDTH + MLA_HEADS * MLA_V
SSM_INNER = 2 * D_MODEL
SSM_HEAD_DIM = 64
SSM_HEADS = SSM_INNER // SSM_HEAD_DIM
SSM_GROUPS = 8
SSM_STATE = 128
SSM_CONV = 5
SSM_CHUNK = 256
SSM_CONV_CH = SSM_INNER + 2 * SSM_GROUPS * SSM_STATE
FFN_HIDDEN = 4 * D_MODEL
EPS = 1e-6

QK_PAD = 2 * V7X_LANES
HEADS_PER_GROUP = SSM_HEADS // SSM_GROUPS
PAIRS_PER_GROUP = HEADS_PER_GROUP * SSM_HEAD_DIM // V7X_LANES
X_BLOCKS = SSM_INNER // V7X_LANES
XBC_BLOCKS = SSM_CONV_CH // V7X_LANES
ZX_WIDTH = SSM_INNER + SSM_CONV_CH
LOG2E = 1.4426950408889634


def _cparams(sem, vmem_mib):
    return pltpu.CompilerParams(dimension_semantics=sem,
                                vmem_limit_bytes=vmem_mib * 1024 * 1024)


def _const_spec(shape):
    nd = len(shape)
    return pl.BlockSpec(shape, lambda *_: (0,) * nd, pipeline_mode=pl.Buffered(1))


def _norm_mod(x, g, sc, sh):
    y = x * lax.rsqrt(jnp.mean(x * x, axis=-1, keepdims=True) + EPS)
    return (y * g) * (1.0 + sc) + sh


def _silu(x):
    return x * jax.nn.sigmoid(x)


def _ada_kernel(c_ref, w_ref, b_ref, o_ref):
    c = c_ref[...]
    cond = jnp.broadcast_to(_silu(c), (8, c.shape[-1]))
    r = jnp.dot(cond, w_ref[0], preferred_element_type=F32,
                precision=lax.Precision.HIGHEST)
    o_ref[0] = r[0:1] + b_ref[0]


def _ada_mod(c, ada_w, ada_b):
    depth, d, n = ada_w.shape
    tn = 1024
    return pl.pallas_call(
        _ada_kernel,
        grid=(depth, n // tn),
        in_specs=[pl.BlockSpec((1, d), lambda l, j: (0, 0)),
                  pl.BlockSpec((1, d, tn), lambda l, j: (l, 0, j)),
                  pl.BlockSpec((1, 1, tn), lambda l, j: (l, 0, j))],
        out_specs=pl.BlockSpec((1, 1, tn), lambda l, j: (l, 0, j)),
        out_shape=jax.ShapeDtypeStruct((depth, 1, n), F32),
        compiler_params=_cparams(("parallel", "parallel"), 40),
        name="ada_mod",
    )(c, ada_w, ada_b.reshape(depth, 1, n))


def _rope_kernel(pos_ref, inv_ref, cos_ref, sin_ref):
    ang = pos_ref[...] * inv_ref[...]
    cos_ref[...] = jnp.cos(ang)
    sin_ref[...] = jnp.sin(ang)


def _rope_tables(pos_col, inv128):
    s = pos_col.shape[0]
    tm = min(s, 2048)
    return pl.pallas_call(
        _rope_kernel,
        grid=(s // tm,),
        in_specs=[pl.BlockSpec((tm, 1), lambda i: (i, 0)),
                  pl.BlockSpec((1, V7X_LANES), lambda i: (0, 0))],
        out_specs=[pl.BlockSpec((tm, V7X_LANES), lambda i: (i, 0))] * 2,
        out_shape=[jax.ShapeDtypeStruct((s, V7X_LANES), F32)] * 2,
        compiler_params=_cparams(("parallel",), 32),
        name="rope_tables",
    )(pos_col, inv128)


def _hyb_in_kernel(x_ref, g_ref, sc_ref, sh_ref, win_ref, lng_ref, sw_ref, sb_ref,
                   qg_ref, kvg_ref, wqa_ref, wqb_ref, wk_ref, wv_ref, cos_ref, sin_ref,
                   a_ref, q_ref, k_ref, v_ref, proj_scr, vn_scr, *, q_scale):
    tm = x_ref.shape[0]
    h = _norm_mod(x_ref[...], g_ref[...], sc_ref[...], sh_ref[...])
    proj_scr[...] = jnp.dot(h.astype(BF16), win_ref[...], preferred_element_type=F32)

    v = jax.nn.gelu(proj_scr[:, SGU_WIDTH:2 * SGU_WIDTH])
    vc = v - jnp.mean(v, axis=-1, keepdims=True)
    vn = vc * lax.rsqrt(jnp.mean(vc * vc, axis=-1, keepdims=True) + EPS) * lng_ref[...]
    vn_scr[...] = vn.astype(BF16)
    for c in range(tm // SGU_CHUNK):
        rows = slice(c * SGU_CHUNK, (c + 1) * SGU_CHUNK)
        for g in range(SGU_GROUPS):
            cols = slice(g * SGU_HEAD, (g + 1) * SGU_HEAD)
            mixed = jnp.dot(sw_ref[g], vn_scr[rows, cols], preferred_element_type=F32)
            u = jax.nn.gelu(proj_scr[rows, cols])
            a_ref[rows, cols] = (u * (mixed + sb_ref[:, cols])).astype(BF16)

    cos = cos_ref[...]
    sin = sin_ref[...]
    lat0 = 2 * SGU_WIDTH
    ql = proj_scr[:, lat0:lat0 + MLA_Q_RANK]
    qn = (ql * lax.rsqrt(jnp.mean(ql * ql, axis=-1, keepdims=True) + EPS) * qg_ref[...]).astype(BF16)
    qa = jnp.dot(qn, wqa_ref[...], preferred_element_type=F32)
    qb = jnp.dot(qn, wqb_ref[...], preferred_element_type=F32)
    for hd in range(MLA_HEADS):
        o = hd * QK_PAD
        q_ref[:, o:o + V7X_LANES] = (qa[:, o:o + V7X_LANES] * q_scale).astype(BF16)
        pe = (qa[:, o + V7X_LANES:o + QK_PAD] * cos
              + qb[:, hd * V7X_LANES:(hd + 1) * V7X_LANES] * sin)
        q_ref[:, o + V7X_LANES:o + QK_PAD] = (pe * q_scale).astype(BF16)

    kv0 = lat0 + MLA_Q_RANK
    kvl = proj_scr[:, kv0:kv0 + MLA_KV_RANK]
    kvn = (kvl * lax.rsqrt(jnp.mean(kvl * kvl, axis=-1, keepdims=True) + EPS) * kvg_ref[...]).astype(BF16)
    kn = jnp.dot(kvn, wk_ref[...], preferred_element_type=F32)
    vv = jnp.dot(kvn, wv_ref[...], preferred_element_type=F32)
    pe0 = kv0 + MLA_KV_RANK
    kpe = (proj_scr[:, pe0:pe0 + V7X_LANES] * cos
           + proj_scr[:, pe0 + V7X_LANES:pe0 + 2 * V7X_LANES] * sin).astype(BF16)
    ones = jnp.ones((tm, V7X_LANES), BF16)
    for hd in range(MLA_HEADS):
        o = hd * QK_PAD
        hs = slice(hd * V7X_LANES, (hd + 1) * V7X_LANES)
        k_ref[:, o:o + V7X_LANES] = kn[:, hs].astype(BF16)
        k_ref[:, o + V7X_LANES:o + QK_PAD] = kpe
        v_ref[:, o:o + V7X_LANES] = vv[:, hs].astype(BF16)
        v_ref[:, o + V7X_LANES:o + QK_PAD] = ones


def _hyb_in(x2, g, sc, sh, win, lng, sw, sb, qg, kvg, wqa, wqb, wk, wv, cos, sin, q_scale):
    s, d = x2.shape
    tm = min(s, 512)
    nproj = win.shape[1]
    row = lambda w: pl.BlockSpec((tm, w), lambda i: (i, 0))
    vec = lambda w: pl.BlockSpec((1, w), lambda i: (0, 0))
    hq = MLA_HEADS * QK_PAD
    return pl.pallas_call(
        functools.partial(_hyb_in_kernel, q_scale=q_scale),
        grid=(s // tm,),
        in_specs=[row(d), vec(d), vec(d), vec(d), _const_spec(win.shape), vec(SGU_WIDTH),
                  _const_spec(sw.shape), _const_spec(sb.shape), vec(MLA_Q_RANK), vec(MLA_KV_RANK),
                  _const_spec(wqa.shape), _const_spec(wqb.shape), _const_spec(wk.shape),
                  _const_spec(wv.shape), row(V7X_LANES), row(V7X_LANES)],
        out_specs=[row(SGU_WIDTH), row(hq), row(hq), row(hq)],
        out_shape=[jax.ShapeDtypeStruct((s, SGU_WIDTH), BF16),
                   jax.ShapeDtypeStruct((s, hq), BF16),
                   jax.ShapeDtypeStruct((s, hq), BF16),
                   jax.ShapeDtypeStruct((s, hq), BF16)],
        scratch_shapes=[pltpu.VMEM((tm, nproj), F32), pltpu.VMEM((tm, SGU_WIDTH), BF16)],
        compiler_params=_cparams(("parallel",), 56),
        name="hyb_in",
    )(x2, g, sc, sh, win, lng, sw, sb, qg, kvg, wqa, wqb, wk, wv, cos, sin)


def _flash_kernel(q_ref, k_ref, v_ref, o_ref, m_scr, acc_scr, *, tk, sub):
    nk = k_ref.shape[0] // tk
    tq = q_ref.shape[0]
    m_scr[...] = jnp.full(m_scr.shape, -jnp.inf, F32)
    acc_scr[...] = jnp.zeros(acc_scr.shape, F32)

    def body(j, carry):
        off = pl.multiple_of(j * tk, tk)
        k = k_ref[pl.ds(off, tk), :]
        v = v_ref[pl.ds(off, tk), :]
        for r in range(tq // sub):
            rows = slice(r * sub, (r + 1) * sub)
            s = lax.dot_general(q_ref[rows, :], k, (((1,), (1,)), ((), ())),
                                preferred_element_type=F32)
            m_prev = m_scr[rows, :]
            m_new = jnp.maximum(m_prev, jnp.max(s, axis=-1, keepdims=True))
            alpha = jnp.exp2(m_prev - m_new)
            p = jnp.exp2(s - jnp.tile(m_new, (1, tk // V7X_LANES)))
            acc_scr[rows, :] = (acc_scr[rows, :] * jnp.tile(alpha, (1, 2))
                                + jnp.dot(p.astype(BF16), v, preferred_element_type=F32))
            m_scr[rows, :] = m_new
        return carry

    lax.fori_loop(0, nk, body, 0, unroll=4 if nk % 4 == 0 else 1)
    acc = acc_scr[...]
    o_ref[...] = (acc[:, :MLA_V] / acc[:, MLA_V:]).astype(BF16)


def _flash(q, k, v):
    s = q.shape[0]
    tq = min(s, 2048)
    tk = min(s, 1024)
    return pl.pallas_call(
        functools.partial(_flash_kernel, tk=tk, sub=min(tq, 256)),
        grid=(MLA_HEADS, s // tq),
        in_specs=[pl.BlockSpec((tq, QK_PAD), lambda h, i: (i, h)),
                  pl.BlockSpec((s, QK_PAD), lambda h, i: (0, h)),
                  pl.BlockSpec((s, QK_PAD), lambda h, i: (0, h))],
        out_specs=pl.BlockSpec((tq, MLA_V), lambda h, i: (i, h)),
        out_shape=jax.ShapeDtypeStruct((s, MLA_HEADS * MLA_V), BF16),
        scratch_shapes=[pltpu.VMEM((tq, V7X_LANES), F32), pltpu.VMEM((tq, QK_PAD), F32)],
        compiler_params=_cparams(("parallel", "arbitrary"), 56),
        name="flash_attn",
    )(q, k, v)


def _hyb_out_kernel(x_ref, a_ref, o_ref, w_ref, gate_ref, y_ref):
    m = (jnp.dot(a_ref[...], w_ref[:SGU_WIDTH, :], preferred_element_type=F32)
         + jnp.dot(o_ref[...], w_ref[SGU_WIDTH:, :], preferred_element_type=F32))
    y_ref[...] = x_ref[...] + gate_ref[...] * m


def _hyb_out(x2, a, o, w, gate):
    s, d = x2.shape
    tm = min(s, 512)
    return pl.pallas_call(
        _hyb_out_kernel,
        grid=(s // tm,),
        in_specs=[pl.BlockSpec((tm, d), lambda i: (i, 0)),
                  pl.BlockSpec((tm, a.shape[1]), lambda i: (i, 0)),
                  pl.BlockSpec((tm, o.shape[1]), lambda i: (i, 0)),
                  _const_spec(w.shape),
                  pl.BlockSpec((1, d), lambda i: (0, 0))],
        out_specs=pl.BlockSpec((tm, d), lambda i: (i, 0)),
        out_shape=jax.ShapeDtypeStruct((s, d), F32),
        compiler_params=_cparams(("parallel",), 48),
        name="hyb_out",
    )(x2, a, o, w, gate)


def _ffn_kernel(x_ref, g_ref, sc_ref, sh_ref, gate_ref, fg_ref, w1_ref, w2_ref, o_ref,
                h_scr, acc_scr, *, final_norm):
    k = pl.program_id(1)

    @pl.when(k == 0)
    def _():
        h_scr[...] = _norm_mod(x_ref[...], g_ref[...], sc_ref[...], sh_ref[...]).astype(BF16)
        acc_scr[...] = jnp.zeros(acc_scr.shape, F32)

    hid = jnp.dot(h_scr[...], w1_ref[...], preferred_element_type=F32)
    hid = jnp.square(jnp.maximum(hid, 0.0)).astype(BF16)
    acc_scr[...] += jnp.dot(hid, w2_ref[...], preferred_element_type=F32)

    @pl.when(k == pl.num_programs(1) - 1)
    def _():
        y = x_ref[...] + gate_ref[...] * acc_scr[...]
        if final_norm:
            y = y * lax.rsqrt(jnp.mean(y * y, axis=-1, keepdims=True) + EPS) * fg_ref[...]
        o_ref[...] = y


def _ffn(x2, g, sc, sh, gate, fg, w1, w2, final_norm):
    s, d = x2.shape
    hidden = w1.shape[1]
    tm = min(s, 512)
    tk = 1024
    vec = pl.BlockSpec((1, d), lambda i, k: (0, 0))
    return pl.pallas_call(
        functools.partial(_ffn_kernel, final_norm=final_norm),
        grid=(s // tm, hidden // tk),
        in_specs=[pl.BlockSpec((tm, d), lambda i, k: (i, 0)), vec, vec, vec, vec, vec,
                  pl.BlockSpec((d, tk), lambda i, k: (0, k)),
                  pl.BlockSpec((tk, d), lambda i, k: (k, 0))],
        out_specs=pl.BlockSpec((tm, d), lambda i, k: (i, 0)),
        out_shape=jax.ShapeDtypeStruct((s, d), F32),
        scratch_shapes=[pltpu.VMEM((tm, d), BF16), pltpu.VMEM((tm, d), F32)],
        compiler_params=_cparams(("parallel", "arbitrary"), 48),
        name="ffn",
    )(x2, g, sc, sh, gate, fg, w1, w2)


def _ssm_in_kernel(x_ref, g_ref, sc_ref, sh_ref, w_ref, wdt_ref, dtb_ref, zx_ref, dt_ref, h_scr):
    @pl.when(pl.program_id(1) == 0)
    def _():
        hb = _norm_mod(x_ref[...], g_ref[...], sc_ref[...], sh_ref[...]).astype(BF16)
        h_scr[...] = hb
        r = jnp.dot(hb, wdt_ref[...], preferred_element_type=F32) + dtb_ref[...]
        dt_ref[...] = jnp.maximum(r, 0.0) + jnp.log1p(jnp.exp(-jnp.abs(r)))

    zx_ref[...] = jnp.dot(h_scr[...], w_ref[...], preferred_element_type=F32).astype(BF16)


def _ssm_in(x2, g, sc, sh, w_zx, w_dt, dt_bias):
    s, d = x2.shape
    n = w_zx.shape[1]
    ndt = w_dt.shape[1]
    tm = min(s, 1024)
    tn = 1024
    vec = pl.BlockSpec((1, d), lambda i, j: (0, 0))
    return pl.pallas_call(
        _ssm_in_kernel,
        grid=(s // tm, n // tn),
        in_specs=[pl.BlockSpec((tm, d), lambda i, j: (i, 0)), vec, vec, vec,
                  pl.BlockSpec((d, tn), lambda i, j: (0, j)),
                  pl.BlockSpec((d, ndt), lambda i, j: (0, 0)),
                  pl.BlockSpec((1, ndt), lambda i, j: (0, 0))],
        out_specs=[pl.BlockSpec((tm, tn), lambda i, j: (i, j)),
                   pl.BlockSpec((tm, ndt), lambda i, j: (i, 0))],
        out_shape=[jax.ShapeDtypeStruct((s, n), BF16), jax.ShapeDtypeStruct((s, ndt), F32)],
        scratch_shapes=[pltpu.VMEM((tm, d), BF16)],
        compiler_params=_cparams(("parallel", "arbitrary"), 48),
        name="ssm_in",
    )(x2, g, sc, sh, w_zx, w_dt, dt_bias)


def _conv_kernel(prev_ref, main_ref, next_ref, w_ref, b_ref, o_ref, ext_scr):
    i = pl.program_id(0)
    tm = main_ref.shape[0]
    halo = prev_ref.shape[0]
    pad = SSM_CONV // 2
    ext_scr[0:halo, :] = jnp.where(i == 0, 0.0, prev_ref[...].astype(F32))
    ext_scr[halo:halo + tm, :] = main_ref[...].astype(F32)
    ext_scr[halo + tm:, :] = jnp.where(i == pl.num_programs(0) - 1, 0.0, next_ref[...].astype(F32))
    acc = jnp.broadcast_to(b_ref[...], (tm, b_ref.shape[1]))
    for t in range(SSM_CONV):
        acc = acc + w_ref[t:t + 1, :] * ext_scr[pl.ds(halo - pad + t, tm), :]
    y = _silu(acc)
    for j in range(o_ref.shape[0]):
        o_ref[j] = y[:, j * V7X_LANES:(j + 1) * V7X_LANES].astype(BF16)


def _conv_silu(zx, conv_w, conv_b):
    s = zx.shape[0]
    tm = min(s, 512)
    cw = 1024
    halo = V7X_BF16_SUBLANE_TILE
    col0 = SSM_INNER // cw
    nrow_h = s // halo
    return pl.pallas_call(
        _conv_kernel,
        grid=(s // tm, SSM_CONV_CH // cw),
        in_specs=[pl.BlockSpec((halo, cw), lambda i, j: (jnp.maximum(i * (tm // halo) - 1, 0), col0 + j)),
                  pl.BlockSpec((tm, cw), lambda i, j: (i, col0 + j)),
                  pl.BlockSpec((halo, cw), lambda i, j: (jnp.minimum((i + 1) * (tm // halo), nrow_h - 1), col0 + j)),
                  pl.BlockSpec((SSM_CONV, cw), lambda i, j: (0, j)),
                  pl.BlockSpec((1, cw), lambda i, j: (0, j))],
        out_specs=pl.BlockSpec((cw // V7X_LANES, tm, V7X_LANES), lambda i, j: (j, i, 0)),
        out_shape=jax.ShapeDtypeStruct((XBC_BLOCKS, s, V7X_LANES), BF16),
        scratch_shapes=[pltpu.VMEM((tm + 2 * halo, cw), F32)],
        compiler_params=_cparams(("parallel", "parallel"), 40),
        name="conv_silu",
    )(zx, zx, zx, conv_w, conv_b)


SSD_GROUPS_PER_STEP = 8


def _ssd_kernel(x_ref, b_ref, c_ref, dtt_ref, alog_ref, y_ref, st_scr, *, reverse):
    L = x_ref.shape[1]
    P = SSM_HEAD_DIM
    ngrp = b_ref.shape[0]
    nh = dtt_ref.shape[0]

    @pl.when(pl.program_id(1) == 0)
    def _():
        st_scr[...] = jnp.zeros(st_scr.shape, F32)

    dtt = dtt_ref[...]
    dta = dtt * (-jnp.exp(alog_ref[...]))
    kk = lax.broadcasted_iota(jnp.int32, (L, L), 0)
    ii = lax.broadcasted_iota(jnp.int32, (L, L), 1)
    tri = jnp.where((kk >= ii) if reverse else (kk <= ii), 1.0, 0.0).astype(BF16)
    hi = dta.astype(BF16)
    r1 = dta - hi.astype(F32)
    mid = r1.astype(BF16)
    lo = (r1 - mid.astype(F32)).astype(BF16)
    parts = jnp.concatenate([hi, mid, lo, jnp.zeros_like(hi)], axis=0)
    cs = jnp.dot(parts, tri, preferred_element_type=F32)
    cum = cs[0:nh] + cs[nh:2 * nh] + cs[2 * nh:3 * nh]
    cum2 = cum * LOG2E
    end = 0 if reverse else L - 1
    cum_end = cum2[:, end:end + 1]
    cum_dt = cum2 - jnp.log(dtt) * LOG2E
    to_end = jnp.exp2(cum_end - cum_dt)
    dec_end = jnp.exp2(cum_end)
    keep = (kk <= ii) if reverse else (kk >= ii)
    low = lax.broadcasted_iota(jnp.int32, (1, V7X_LANES), 1) < P
    heads_per_block = V7X_LANES // P

    for g in range(ngrp):
        cg = c_ref[g]
        bt = b_ref[g].astype(F32).T.astype(BF16)
        cb = jnp.dot(cg, bt, preferred_element_type=F32).astype(BF16)
        gl = slice(g * HEADS_PER_GROUP * P, (g + 1) * HEADS_PER_GROUP * P)
        st = st_scr[:, gl]
        yst = jnp.dot(cg, st.astype(BF16), preferred_element_type=F32)
        for q in range(PAIRS_PER_GROUP):
            blk = g * PAIRS_PER_GROUP + q
            xp = x_ref[blk]
            lanes = slice(q * V7X_LANES, (q + 1) * V7X_LANES)
            ys = []
            sus = []
            for e in range(heads_per_block):
                hd = blk * heads_per_block + e
                cum_col = jnp.broadcast_to(cum2[hd:hd + 1, :], (V7X_LANES, L)).T
                seg = jnp.tile(cum_col, (1, L // V7X_LANES)) - cum_dt[hd:hd + 1, :]
                w = jnp.exp2(jnp.where(keep, seg, -jnp.inf)).astype(BF16) * cb
                yh = jnp.dot(w, xp, preferred_element_type=F32)
                ys.append(yh + yst[:, lanes] * jnp.exp2(cum_col))
                bts = bt * to_end[hd:hd + 1, :].astype(BF16)
                sus.append(jnp.dot(bts, xp, preferred_element_type=F32))
            y_ref[blk] = jnp.where(low, ys[0], ys[1]).astype(BF16)
            h0 = blk * heads_per_block
            dec = jnp.where(low, dec_end[h0:h0 + 1, :], dec_end[h0 + 1:h0 + 2, :])
            sl = slice(blk * V7X_LANES, (blk + 1) * V7X_LANES)
            st_scr[:, sl] = st[:, lanes] * dec + jnp.where(low, sus[0], sus[1])


def _ssd_scan(xbc_t, dtt, alog_col, direction):
    s = xbc_t.shape[1]
    L = SSM_CHUNK
    nc = s // L
    gps = SSD_GROUPS_PER_STEP
    nsteps = SSM_GROUPS // gps
    reverse = direction == 1
    cidx = (lambda c: nc - 1 - c) if reverse else (lambda c: c)
    hrow = direction * nsteps
    nblk = gps * PAIRS_PER_GROUP
    nh = gps * HEADS_PER_GROUP
    return pl.pallas_call(
        functools.partial(_ssd_kernel, reverse=reverse),
        grid=(nsteps, nc),
        in_specs=[pl.BlockSpec((nblk, L, V7X_LANES), lambda g, c: (g, cidx(c), 0)),
                  pl.BlockSpec((gps, L, V7X_LANES), lambda g, c: (X_BLOCKS // gps + g, cidx(c), 0)),
                  pl.BlockSpec((gps, L, V7X_LANES), lambda g, c: ((X_BLOCKS + SSM_GROUPS) // gps + g, cidx(c), 0)),
                  pl.BlockSpec((nh, L), lambda g, c: (hrow + g, cidx(c))),
                  pl.BlockSpec((nh, 1), lambda g, c: (hrow + g, 0))],
        out_specs=pl.BlockSpec((nblk, L, V7X_LANES), lambda g, c: (g, cidx(c), 0)),
        out_shape=jax.ShapeDtypeStruct((X_BLOCKS, s, V7X_LANES), BF16),
        scratch_shapes=[pltpu.VMEM((SSM_STATE, nh * SSM_HEAD_DIM), F32)],
        compiler_params=_cparams(("parallel", "arbitrary"), 40),
        name="ssd_bwd" if reverse else "ssd_fwd",
    )(xbc_t, xbc_t, xbc_t, dtt, alog_col)


def _ssm_out_kernel(x_ref, yf_ref, yb_ref, xs_ref, z_ref, d_ref, ng_ref, w_ref, gate_ref, o_ref):
    per_group = PAIRS_PER_GROUP
    gw = SSM_INNER // SSM_GROUPS
    pieces = []
    for g in range(SSM_GROUPS):
        ys = []
        ss = None
        for p in range(per_group):
            j = g * per_group + p
            lanes = slice(j * V7X_LANES, (j + 1) * V7X_LANES)
            y = (yf_ref[j].astype(F32) + yb_ref[j].astype(F32)
                 + d_ref[:, lanes] * xs_ref[j].astype(F32))
            y = y * _silu(z_ref[:, lanes].astype(F32))
            ys.append(y)
            t = jnp.sum(y * y, axis=-1, keepdims=True)
            ss = t if ss is None else ss + t
        scale = lax.rsqrt(ss * (1.0 / gw) + EPS)
        for p in range(per_group):
            j = g * per_group + p
            lanes = slice(j * V7X_LANES, (j + 1) * V7X_LANES)
            pieces.append((ys[p] * scale * ng_ref[:, lanes]).astype(BF16))
    yn = jnp.concatenate(pieces, axis=1)
    o_ref[...] = x_ref[...] + gate_ref[...] * jnp.dot(yn, w_ref[...], preferred_element_type=F32)


def _ssm_out(x2, yf, yb, xbc_t, zx, d_full, ng, w, gate):
    s, d = x2.shape
    tm = min(s, 256)
    yspec = pl.BlockSpec((X_BLOCKS, tm, V7X_LANES), lambda i: (0, i, 0))
    return pl.pallas_call(
        _ssm_out_kernel,
        grid=(s // tm,),
        in_specs=[pl.BlockSpec((tm, d), lambda i: (i, 0)), yspec, yspec, yspec,
                  pl.BlockSpec((tm, SSM_INNER), lambda i: (i, 0)),
                  pl.BlockSpec((1, SSM_INNER), lambda i: (0, 0)),
                  pl.BlockSpec((1, SSM_INNER), lambda i: (0, 0)),
                  _const_spec(w.shape),
                  pl.BlockSpec((1, d), lambda i: (0, 0))],
        out_specs=pl.BlockSpec((tm, d), lambda i: (i, 0)),
        out_shape=jax.ShapeDtypeStruct((s, d), F32),
        compiler_params=_cparams(("parallel",), 56),
        name="ssm_out",
    )(x2, yf, yb, xbc_t, zx, d_full, ng, w, gate)


def _rot_cols(w):
    half = MLA_ROPE // 2
    return jnp.concatenate([-w[..., half:], w[..., :half]], axis=-1)


def _pad_lanes(w, width):
    return jnp.pad(w, [(0, 0)] * (w.ndim - 1) + [(0, width - w.shape[-1])])


def kernel(x, c, positions, ada_w, ada_b, norm_mix_g, norm_ffn_g, ffn_w1, ffn_w2, hyb_w_in, sgu_norm_g, sgu_w, sgu_b, mla_q_norm_g, mla_kv_norm_g, mla_w_uq, mla_w_ukv, hyb_w_out, ssm_w_in, ssm_conv_w, ssm_conv_b, ssm_dt_bias, ssm_a_log, ssm_d, ssm_norm_g, ssm_w_out, final_norm_g):
    batch, s, d = x.shape
    assert batch == 1 and d == D_MODEL and s % 1024 == 0
    depth = ada_w.shape[0]
    assert depth == 2
    x2 = x.reshape(s, d)

    mod = _ada_mod(c, ada_w, ada_b)
    mods = [[mod[l, :, i * d:(i + 1) * d] for i in range(6)] for l in range(depth)]
    row = lambda v: v.reshape(1, -1)

    sh1, sc1, g1, sh2, sc2, g2 = mods[0]
    w_in = hyb_w_in[0]
    c_pe = 2 * SGU_WIDTH + MLA_Q_RANK + MLA_KV_RANK
    w_kpe = w_in[:, c_pe:]
    win_ext = jnp.concatenate(
        [w_in[:, :c_pe], _pad_lanes(w_kpe, V7X_LANES), _pad_lanes(_rot_cols(w_kpe), V7X_LANES)],
        axis=1).astype(BF16)
    wq = mla_w_uq[0].reshape(MLA_Q_RANK, MLA_HEADS, MLA_NOPE + MLA_ROPE)
    wqa = _pad_lanes(wq, QK_PAD).reshape(MLA_Q_RANK, MLA_HEADS * QK_PAD).astype(BF16)
    wqb = _pad_lanes(_rot_cols(wq[..., MLA_NOPE:]), V7X_LANES).reshape(
        MLA_Q_RANK, MLA_HEADS * V7X_LANES).astype(BF16)
    wkv = mla_w_ukv[0].reshape(MLA_KV_RANK, MLA_HEADS, MLA_NOPE + MLA_V)
    wk = wkv[..., :MLA_NOPE].reshape(MLA_KV_RANK, MLA_HEADS * MLA_NOPE).astype(BF16)
    wv = wkv[..., MLA_NOPE:].reshape(MLA_KV_RANK, MLA_HEADS * MLA_V).astype(BF16)
    sb_full = jnp.repeat(sgu_b[0].T, SGU_HEAD, axis=1)

    half = MLA_ROPE // 2
    inv_freq = ROPE_THETA ** (-jnp.arange(half, dtype=F32) / half)
    inv128 = jnp.tile(inv_freq, V7X_LANES // half).reshape(1, V7X_LANES)
    cos, sin = _rope_tables(positions.astype(F32).reshape(s, 1), inv128)
    q_scale = (MLA_NOPE + MLA_ROPE) ** -0.5 * LOG2E

    a_out, q, k, v = _hyb_in(x2, row(norm_mix_g[0]), sc1, sh1, win_ext, row(sgu_norm_g[0]),
                             sgu_w[0].astype(BF16), sb_full, row(mla_q_norm_g[0]),
                             row(mla_kv_norm_g[0]), wqa, wqb, wk, wv, cos, sin, q_scale)
    o = _flash(q, k, v)
    x2 = _hyb_out(x2, a_out, o, hyb_w_out[0].astype(BF16), g1)
    x2 = _ffn(x2, row(norm_ffn_g[0]), sc2, sh2, g2, row(final_norm_g),
              ffn_w1[0].astype(BF16), ffn_w2[0].astype(BF16), False)

    sh1, sc1, g1, sh2, sc2, g2 = mods[1]
    w_in = ssm_w_in[0]
    zx, dt = _ssm_in(x2, row(norm_mix_g[1]), sc1, sh1, w_in[:, :ZX_WIDTH].astype(BF16),
                     w_in[:, ZX_WIDTH:].astype(BF16), ssm_dt_bias[0].reshape(1, 2 * SSM_HEADS))
    xbc_t = _conv_silu(zx, ssm_conv_w[0], row(ssm_conv_b[0]))
    dtt = dt.T
    alog_col = ssm_a_log[0].reshape(2 * SSM_HEADS, 1)
    yf = _ssd_scan(xbc_t, dtt, alog_col, 0)
    yb = _ssd_scan(xbc_t, dtt, alog_col, 1)
    d_full = jnp.repeat(ssm_d[0], SSM_HEAD_DIM).reshape(1, SSM_INNER)
    x2 = _ssm_out(x2, yf, yb, xbc_t, zx, d_full, row(ssm_norm_g[0]), ssm_w_out[0].astype(BF16), g1)
    x2 = _ffn(x2, row(norm_ffn_g[1]), sc2, sh2, g2, row(final_norm_g),
              ffn_w1[1].astype(BF16), ffn_w2[1].astype(BF16), True)
    return x2.reshape(batch, s, d)
```

```python
import functools
import math

import jax
import jax.numpy as jnp
from jax import lax
from jax.experimental import pallas as pl
from jax.experimental.pallas import tpu as pltpu

F32 = jnp.float32
BF16 = jnp.bfloat16

V7X_LANES = 128
V7X_BF16_SUBLANE_TILE = 16
V7X_VMEM_BYTES = 64 * 1024 * 1024

D_MODEL = 2048
SGU_CHUNK = 128
SGU_GROUPS = 8
SGU_HEAD = 128
SGU_WIDTH = SGU_GROUPS * SGU_HEAD
MLA_HEADS = 8
MLA_Q_RANK = 512
MLA_KV_RANK = 512
MLA_NOPE = 128
MLA_ROPE = 64
MLA_V = 128
ROPE_THETA = 10000.0
HYB_MIX = SGU_WIDTH + MLA_HEADS * MLA_V
SSM_INNER = 2 * D_MODEL
SSM_HEAD_DIM = 64
SSM_HEADS = SSM_INNER // SSM_HEAD_DIM
SSM_GROUPS = 8
SSM_STATE = 128
SSM_CONV = 5
SSM_CHUNK = 256
SSM_CONV_CH = SSM_INNER + 2 * SSM_GROUPS * SSM_STATE
FFN_HIDDEN = 4 * D_MODEL
EPS = 1e-6

QK_PAD = 2 * V7X_LANES
HEADS_PER_GROUP = SSM_HEADS // SSM_GROUPS
PAIRS_PER_GROUP = HEADS_PER_GROUP * SSM_HEAD_DIM // V7X_LANES
X_BLOCKS = SSM_INNER // V7X_LANES
XBC_BLOCKS = SSM_CONV_CH // V7X_LANES
ZX_WIDTH = SSM_INNER + SSM_CONV_CH
LOG2E = 1.4426950408889634


def _cparams(sem, vmem_mib):
    return pltpu.CompilerParams(dimension_semantics=sem,
                                vmem_limit_bytes=vmem_mib * 1024 * 1024)


def _const_spec(shape):
    nd = len(shape)
    return pl.BlockSpec(shape, lambda *_: (0,) * nd, pipeline_mode=pl.Buffered(1))


def _norm_mod(x, g, sc, sh):
    y = x * lax.rsqrt(jnp.mean(x * x, axis=-1, keepdims=True) + EPS)
    return (y * g) * (1.0 + sc) + sh


def _silu(x):
    return x * jax.nn.sigmoid(x)


def _ada_kernel(c_ref, w_ref, b_ref, o_ref):
    c = c_ref[...]
    cond = jnp.broadcast_to(_silu(c), (8, c.shape[-1]))
    r = jnp.dot(cond, w_ref[0], preferred_element_type=F32,
                precision=lax.Precision.HIGHEST)
    o_ref[0] = r[0:1] + b_ref[0]


def _ada_mod(c, ada_w, ada_b):
    depth, d, n = ada_w.shape
    tn = 1024
    return pl.pallas_call(
        _ada_kernel,
        grid=(depth, n // tn),
        in_specs=[pl.BlockSpec((1, d), lambda l, j: (0, 0)),
                  pl.BlockSpec((1, d, tn), lambda l, j: (l, 0, j)),
                  pl.BlockSpec((1, 1, tn), lambda l, j: (l, 0, j))],
        out_specs=pl.BlockSpec((1, 1, tn), lambda l, j: (l, 0, j)),
        out_shape=jax.ShapeDtypeStruct((depth, 1, n), F32),
        compiler_params=_cparams(("parallel", "parallel"), 40),
        name="ada_mod",
    )(c, ada_w, ada_b.reshape(depth, 1, n))


def _rope_kernel(pos_ref, inv_ref, cos_ref, sin_ref):
    ang = pos_ref[...] * inv_ref[...]
    cos_ref[...] = jnp.cos(ang)
    sin_ref[...] = jnp.sin(ang)


def _rope_tables(pos_col, inv128):
    s = pos_col.shape[0]
    tm = min(s, 2048)
    return pl.pallas_call(
        _rope_kernel,
        grid=(s // tm,),
        in_specs=[pl.BlockSpec((tm, 1), lambda i: (i, 0)),
                  pl.BlockSpec((1, V7X_LANES), lambda i: (0, 0))],
        out_specs=[pl.BlockSpec((tm, V7X_LANES), lambda i: (i, 0))] * 2,
        out_shape=[jax.ShapeDtypeStruct((s, V7X_LANES), F32)] * 2,
        compiler_params=_cparams(("parallel",), 32),
        name="rope_tables",
    )(pos_col, inv128)


def _hyb_in_kernel(x_ref, g_ref, sc_ref, sh_ref, win_ref, lng_ref, sw_ref, sb_ref,
                   qg_ref, kvg_ref, wqa_ref, wqb_ref, wk_ref, wv_ref, cos_ref, sin_ref,
                   a_ref, q_ref, k_ref, v_ref, proj_scr, vn_scr, *, q_scale):
    tm = x_ref.shape[0]
    h = _norm_mod(x_ref[...], g_ref[...], sc_ref[...], sh_ref[...])
    proj_scr[...] = jnp.dot(h.astype(BF16), win_ref[...], preferred_element_type=F32)

    v = jax.nn.gelu(proj_scr[:, SGU_WIDTH:2 * SGU_WIDTH])
    vc = v - jnp.mean(v, axis=-1, keepdims=True)
    vn = vc * lax.rsqrt(jnp.mean(vc * vc, axis=-1, keepdims=True) + EPS) * lng_ref[...]
    vn_scr[...] = vn.astype(BF16)
    for c in range(tm // SGU_CHUNK):
        rows = slice(c * SGU_CHUNK, (c + 1) * SGU_CHUNK)
        for g in range(SGU_GROUPS):
            cols = slice(g * SGU_HEAD, (g + 1) * SGU_HEAD)
            mixed = jnp.dot(sw_ref[g], vn_scr[rows, cols], preferred_element_type=F32)
            u = jax.nn.gelu(proj_scr[rows, cols])
            a_ref[rows, cols] = (u * (mixed + sb_ref[:, cols])).astype(BF16)

    cos = cos_ref[...]
    sin = sin_ref[...]
    lat0 = 2 * SGU_WIDTH
    ql = proj_scr[:, lat0:lat0 + MLA_Q_RANK]
    qn = (ql * lax.rsqrt(jnp.mean(ql * ql, axis=-1, keepdims=True) + EPS) * qg_ref[...]).astype(BF16)
    qa = jnp.dot(qn, wqa_ref[...], preferred_element_type=F32)
    qb = jnp.dot(qn, wqb_ref[...], preferred_element_type=F32)
    for hd in range(MLA_HEADS):
        o = hd * QK_PAD
        q_ref[:, o:o + V7X_LANES] = (qa[:, o:o + V7X_LANES] * q_scale).astype(BF16)
        pe = (qa[:, o + V7X_LANES:o + QK_PAD] * cos
              + qb[:, hd * V7X_LANES:(hd + 1) * V7X_LANES] * sin)
        q_ref[:, o + V7X_LANES:o + QK_PAD] = (pe * q_scale).astype(BF16)

    kv0 = lat0 + MLA_Q_RANK
    kvl = proj_scr[:, kv0:kv0 + MLA_KV_RANK]
    kvn = (kvl * lax.rsqrt(jnp.mean(kvl * kvl, axis=-1, keepdims=True) + EPS) * kvg_ref[...]).astype(BF16)
    kn = jnp.dot(kvn, wk_ref[...], preferred_element_type=F32)
    vv = jnp.dot(kvn, wv_ref[...], preferred_element_type=F32)
    pe0 = kv0 + MLA_KV_RANK
    kpe = (proj_scr[:, pe0:pe0 + V7X_LANES] * cos
           + proj_scr[:, pe0 + V7X_LANES:pe0 + 2 * V7X_LANES] * sin).astype(BF16)
    ones = jnp.ones((tm, V7X_LANES), BF16)
    for hd in range(MLA_HEADS):
        o = hd * QK_PAD
        hs = slice(hd * V7X_LANES, (hd + 1) * V7X_LANES)
        k_ref[:, o:o + V7X_LANES] = kn[:, hs].astype(BF16)
        k_ref[:, o + V7X_LANES:o + QK_PAD] = kpe
        v_ref[:, o:o + V7X_LANES] = vv[:, hs].astype(BF16)
        v_ref[:, o + V7X_LANES:o + QK_PAD] = ones


def _hyb_in(x2, g, sc, sh, win, lng, sw, sb, qg, kvg, wqa, wqb, wk, wv, cos, sin, q_scale):
    s, d = x2.shape
    tm = min(s, 512)
    nproj = win.shape[1]
    row = lambda w: pl.BlockSpec((tm, w), lambda i: (i, 0))
    vec = lambda w: pl.BlockSpec((1, w), lambda i: (0, 0))
    hq = MLA_HEADS * QK_PAD
    return pl.pallas_call(
        functools.partial(_hyb_in_kernel, q_scale=q_scale),
        grid=(s // tm,),
        in_specs=[row(d), vec(d), vec(d), vec(d), _const_spec(win.shape), vec(SGU_WIDTH),
                  _const_spec(sw.shape), _const_spec(sb.shape), vec(MLA_Q_RANK), vec(MLA_KV_RANK),
                  _const_spec(wqa.shape), _const_spec(wqb.shape), _const_spec(wk.shape),
                  _const_spec(wv.shape), row(V7X_LANES), row(V7X_LANES)],
        out_specs=[row(SGU_WIDTH), row(hq), row(hq), row(hq)],
        out_shape=[jax.ShapeDtypeStruct((s, SGU_WIDTH), BF16),
                   jax.ShapeDtypeStruct((s, hq), BF16),
                   jax.ShapeDtypeStruct((s, hq), BF16),
                   jax.ShapeDtypeStruct((s, hq), BF16)],
        scratch_shapes=[pltpu.VMEM((tm, nproj), F32), pltpu.VMEM((tm, SGU_WIDTH), BF16)],
        compiler_params=_cparams(("parallel",), 56),
        name="hyb_in",
    )(x2, g, sc, sh, win, lng, sw, sb, qg, kvg, wqa, wqb, wk, wv, cos, sin)


def _flash_kernel(q_ref, k_ref, v_ref, o_ref, m_scr, acc_scr, *, tk, sub):
    nk = k_ref.shape[0] // tk
    tq = q_ref.shape[0]
    m_scr[...] = jnp.full(m_scr.shape, -jnp.inf, F32)
    acc_scr[...] = jnp.zeros(acc_scr.shape, F32)

    def body(j, carry):
        off = pl.multiple_of(j * tk, tk)
        k = k_ref[pl.ds(off, tk), :]
        v = v_ref[pl.ds(off, tk), :]
        for r in range(tq // sub):
            rows = slice(r * sub, (r + 1) * sub)
            s = lax.dot_general(q_ref[rows, :], k, (((1,), (1,)), ((), ())),
                                preferred_element_type=F32)
            m_prev = m_scr[rows, :]
            m_new = jnp.maximum(m_prev, jnp.max(s, axis=-1, keepdims=True))
            alpha = jnp.exp2(m_prev - m_new)
            p = jnp.exp2(s - jnp.tile(m_new, (1, tk // V7X_LANES)))
            acc_scr[rows, :] = (acc_scr[rows, :] * jnp.tile(alpha, (1, 2))
                                + jnp.dot(p.astype(BF16), v, preferred_element_type=F32))
            m_scr[rows, :] = m_new
        return carry

    lax.fori_loop(0, nk, body, 0, unroll=4 if nk % 4 == 0 else 1)
    acc = acc_scr[...]
    o_ref[...] = (acc[:, :MLA_V] / acc[:, MLA_V:]).astype(BF16)


def _flash(q, k, v):
    s = q.shape[0]
    tq = min(s, 2048)
    tk = min(s, 1024)
    return pl.pallas_call(
        functools.partial(_flash_kernel, tk=tk, sub=min(tq, 256)),
        grid=(MLA_HEADS, s // tq),
        in_specs=[pl.BlockSpec((tq, QK_PAD), lambda h, i: (i, h)),
                  pl.BlockSpec((s, QK_PAD), lambda h, i: (0, h)),
                  pl.BlockSpec((s, QK_PAD), lambda h, i: (0, h))],
        out_specs=pl.BlockSpec((tq, MLA_V), lambda h, i: (i, h)),
        out_shape=jax.ShapeDtypeStruct((s, MLA_HEADS * MLA_V), BF16),
        scratch_shapes=[pltpu.VMEM((tq, V7X_LANES), F32), pltpu.VMEM((tq, QK_PAD), F32)],
        compiler_params=_cparams(("parallel", "arbitrary"), 56),
        name="flash_attn",
    )(q, k, v)


def _hyb_out_kernel(x_ref, a_ref, o_ref, w_ref, gate_ref, y_ref):
    m = (jnp.dot(a_ref[...], w_ref[:SGU_WIDTH, :], preferred_element_type=F32)
         + jnp.dot(o_ref[...], w_ref[SGU_WIDTH:, :], preferred_element_type=F32))
    y_ref[...] = x_ref[...] + gate_ref[...] * m


def _hyb_out(x2, a, o, w, gate):
    s, d = x2.shape
    tm = min(s, 512)
    return pl.pallas_call(
        _hyb_out_kernel,
        grid=(s // tm,),
        in_specs=[pl.BlockSpec((tm, d), lambda i: (i, 0)),
                  pl.BlockSpec((tm, a.shape[1]), lambda i: (i, 0)),
                  pl.BlockSpec((tm, o.shape[1]), lambda i: (i, 0)),
                  _const_spec(w.shape),
                  pl.BlockSpec((1, d), lambda i: (0, 0))],
        out_specs=pl.BlockSpec((tm, d), lambda i: (i, 0)),
        out_shape=jax.ShapeDtypeStruct((s, d), F32),
        compiler_params=_cparams(("parallel",), 48),
        name="hyb_out",
    )(x2, a, o, w, gate)


def _ffn_kernel(x_ref, g_ref, sc_ref, sh_ref, gate_ref, fg_ref, w1_ref, w2_ref, o_ref,
                h_scr, acc_scr, *, final_norm):
    k = pl.program_id(1)

    @pl.when(k == 0)
    def _():
        h_scr[...] = _norm_mod(x_ref[...], g_ref[...], sc_ref[...], sh_ref[...]).astype(BF16)
        acc_scr[...] = jnp.zeros(acc_scr.shape, F32)

    hid = jnp.dot(h_scr[...], w1_ref[...], preferred_element_type=F32)
    hid = jnp.square(jnp.maximum(hid, 0.0)).astype(BF16)
    acc_scr[...] += jnp.dot(hid, w2_ref[...], preferred_element_type=F32)

    @pl.when(k == pl.num_programs(1) - 1)
    def _():
        y = x_ref[...] + gate_ref[...] * acc_scr[...]
        if final_norm:
            y = y * lax.rsqrt(jnp.mean(y * y, axis=-1, keepdims=True) + EPS) * fg_ref[...]
        o_ref[...] = y


def _ffn(x2, g, sc, sh, gate, fg, w1, w2, layer, final_norm):
    s, d = x2.shape
    hidden = w1.shape[2]
    tm = min(s, 512)
    tk = 1024
    vec = pl.BlockSpec((1, d), lambda i, k: (0, 0))
    return pl.pallas_call(
        functools.partial(_ffn_kernel, final_norm=final_norm),
        grid=(s // tm, hidden // tk),
        in_specs=[pl.BlockSpec((tm, d), lambda i, k: (i, 0)), vec, vec, vec, vec, vec,
                  pl.BlockSpec((None, d, tk), lambda i, k: (layer, 0, k)),
                  pl.BlockSpec((None, tk, d), lambda i, k: (layer, k, 0))],
        out_specs=pl.BlockSpec((tm, d), lambda i, k: (i, 0)),
        out_shape=jax.ShapeDtypeStruct((s, d), F32),
        scratch_shapes=[pltpu.VMEM((tm, d), BF16), pltpu.VMEM((tm, d), F32)],
        compiler_params=_cparams(("parallel", "arbitrary"), 48),
        name="ffn",
    )(x2, g, sc, sh, gate, fg, w1, w2)


SSM_IN_HALO = V7X_BF16_SUBLANE_TILE
SSM_IN_SUB = 256
SSM_IN_ROWS = 1024


def _ssm_in_kernel(xp_ref, x_ref, xn_ref, g_ref, sc_ref, sh_ref, w_ref, wdt_ref, dtb_ref,
                   cw_ref, cb_ref, z_ref, xbc_ref, dtt_ref, h_scr, *, nz):
    i = pl.program_id(0)
    j = pl.program_id(1)
    tm = x_ref.shape[0]
    halo = xp_ref.shape[0]
    pad = SSM_CONV // 2

    @pl.when(j == 0)
    def _():
        norm = lambda v: _norm_mod(v, g_ref[...], sc_ref[...], sh_ref[...])
        hb = norm(x_ref[...]).astype(BF16)
        h_scr[0:halo, :] = jnp.where(i == 0, 0.0, norm(xp_ref[...])).astype(BF16)
        h_scr[halo:halo + tm, :] = hb
        h_scr[halo + tm:, :] = jnp.where(i == pl.num_programs(0) - 1, 0.0, norm(xn_ref[...])).astype(BF16)
        r = jnp.dot(hb, wdt_ref[...], preferred_element_type=F32) + dtb_ref[...]
        dtt_ref[...] = (jnp.maximum(r, 0.0) + jnp.log1p(jnp.exp(-jnp.abs(r)))).T

    @pl.when(j < nz)
    def _():
        z_ref[...] = jnp.dot(h_scr[halo:halo + tm, :], w_ref[...],
                             preferred_element_type=F32).astype(BF16)

    @pl.when(j >= nz)
    def _():
        rb = min(tm, SSM_IN_ROWS)
        for r in range(tm // rb):
            for c in range(w_ref.shape[1] // SSM_IN_SUB):
                cols = slice(c * SSM_IN_SUB, (c + 1) * SSM_IN_SUB)
                pr = jnp.dot(h_scr[r * rb:r * rb + rb + 2 * halo, :], w_ref[:, cols],
                             preferred_element_type=F32)
                acc = jnp.broadcast_to(cb_ref[:, cols], (rb, SSM_IN_SUB))
                for t in range(SSM_CONV):
                    acc = acc + cw_ref[t:t + 1, cols] * pr[halo - pad + t:halo - pad + t + rb, :]
                y = _silu(acc)
                for u in range(SSM_IN_SUB // V7X_LANES):
                    xbc_ref[c * (SSM_IN_SUB // V7X_LANES) + u, r * rb:(r + 1) * rb, :] = (
                        y[:, u * V7X_LANES:(u + 1) * V7X_LANES].astype(BF16))


def _ssm_in(x2, g, sc, sh, w_in, dt_bias, conv_w, conv_b):
    s, d = x2.shape
    ndt = 2 * SSM_HEADS
    tm = min(s, 1024)
    tn = 1024
    halo = SSM_IN_HALO
    nz = SSM_INNER // tn
    nrow_h = s // halo
    vec = pl.BlockSpec((1, d), lambda i, j: (0, 0))
    ccol = lambda j: jnp.maximum(j - nz, 0)
    return pl.pallas_call(
        functools.partial(_ssm_in_kernel, nz=nz),
        grid=(s // tm, ZX_WIDTH // tn),
        in_specs=[pl.BlockSpec((halo, d), lambda i, j: (jnp.maximum(i * (tm // halo) - 1, 0), 0)),
                  pl.BlockSpec((tm, d), lambda i, j: (i, 0)),
                  pl.BlockSpec((halo, d), lambda i, j: (jnp.minimum((i + 1) * (tm // halo), nrow_h - 1), 0)),
                  vec, vec, vec,
                  pl.BlockSpec((d, tn), lambda i, j: (0, j)),
                  pl.BlockSpec((d, ndt), lambda i, j: (0, ZX_WIDTH // ndt)),
                  pl.BlockSpec((1, ndt), lambda i, j: (0, 0)),
                  pl.BlockSpec((SSM_CONV, tn), lambda i, j: (0, ccol(j))),
                  pl.BlockSpec((1, tn), lambda i, j: (0, ccol(j)))],
        out_specs=[pl.BlockSpec((tm, tn), lambda i, j: (i, jnp.minimum(j, nz - 1))),
                   pl.BlockSpec((tn // V7X_LANES, tm, V7X_LANES), lambda i, j: (ccol(j), i, 0)),
                   pl.BlockSpec((ndt, tm), lambda i, j: (0, i))],
        out_shape=[jax.ShapeDtypeStruct((s, SSM_INNER), BF16),
                   jax.ShapeDtypeStruct((XBC_BLOCKS, s, V7X_LANES), BF16),
                   jax.ShapeDtypeStruct((ndt, s), F32)],
        scratch_shapes=[pltpu.VMEM((tm + 2 * halo, d), BF16)],
        compiler_params=_cparams(("parallel", "arbitrary"), 48),
        name="ssm_in",
    )(x2, x2, x2, g, sc, sh, w_in, w_in, dt_bias, conv_w, conv_b)


SSD_GROUPS_PER_STEP = 8


def _ssd_kernel(x_ref, b_ref, c_ref, dtt_ref, alog_ref, y_ref, st_scr, *, reverse):
    L = x_ref.shape[1]
    P = SSM_HEAD_DIM
    ngrp = b_ref.shape[0]
    nh = dtt_ref.shape[0]

    @pl.when(pl.program_id(1) == 0)
    def _():
        st_scr[...] = jnp.zeros(st_scr.shape, F32)

    dtt = dtt_ref[...]
    dta = dtt * (-jnp.exp(alog_ref[...]))
    kk = lax.broadcasted_iota(jnp.int32, (L, L), 0)
    ii = lax.broadcasted_iota(jnp.int32, (L, L), 1)
    tri = jnp.where((kk >= ii) if reverse else (kk <= ii), 1.0, 0.0).astype(BF16)
    hi = dta.astype(BF16)
    r1 = dta - hi.astype(F32)
    mid = r1.astype(BF16)
    lo = (r1 - mid.astype(F32)).astype(BF16)
    parts = jnp.concatenate([hi, mid, lo, jnp.zeros_like(hi)], axis=0)
    cs = jnp.dot(parts, tri, preferred_element_type=F32)
    cum = cs[0:nh] + cs[nh:2 * nh] + cs[2 * nh:3 * nh]
    cum2 = cum * LOG2E
    end = 0 if reverse else L - 1
    cum_end = cum2[:, end:end + 1]
    cum_dt = cum2 - jnp.log(dtt) * LOG2E
    to_end = jnp.exp2(cum_end - cum_dt)
    dec_end = jnp.exp2(cum_end)
    keep = (kk <= ii) if reverse else (kk >= ii)
    low = lax.broadcasted_iota(jnp.int32, (1, V7X_LANES), 1) < P
    heads_per_block = V7X_LANES // P

    for g in range(ngrp):
        cg = c_ref[g]
        bt = b_ref[g].astype(F32).T.astype(BF16)
        cb = jnp.dot(cg, bt, preferred_element_type=F32).astype(BF16)
        gl = slice(g * HEADS_PER_GROUP * P, (g + 1) * HEADS_PER_GROUP * P)
        st = st_scr[:, gl]
        yst = jnp.dot(cg, st.astype(BF16), preferred_element_type=F32)
        for q in range(PAIRS_PER_GROUP):
            blk = g * PAIRS_PER_GROUP + q
            xp = x_ref[blk]
            lanes = slice(q * V7X_LANES, (q + 1) * V7X_LANES)
            ys = []
            sus = []
            for e in range(heads_per_block):
                hd = blk * heads_per_block + e
                cum_col = jnp.broadcast_to(cum2[hd:hd + 1, :], (V7X_LANES, L)).T
                seg = jnp.tile(cum_col, (1, L // V7X_LANES)) - cum_dt[hd:hd + 1, :]
                w = jnp.exp2(jnp.where(keep, seg, -jnp.inf)).astype(BF16) * cb
                yh = jnp.dot(w, xp, preferred_element_type=F32)
                ys.append(yh + yst[:, lanes] * jnp.exp2(cum_col))
                bts = bt * to_end[hd:hd + 1, :].astype(BF16)
                sus.append(jnp.dot(bts, xp, preferred_element_type=F32))
            y_ref[blk] = jnp.where(low, ys[0], ys[1]).astype(BF16)
            h0 = blk * heads_per_block
            dec = jnp.where(low, dec_end[h0:h0 + 1, :], dec_end[h0 + 1:h0 + 2, :])
            sl = slice(blk * V7X_LANES, (blk + 1) * V7X_LANES)
            st_scr[:, sl] = st[:, lanes] * dec + jnp.where(low, sus[0], sus[1])


def _ssd_scan(xbc_t, dtt, alog_col, direction):
    s = xbc_t.shape[1]
    L = SSM_CHUNK
    nc = s // L
    gps = SSD_GROUPS_PER_STEP
    nsteps = SSM_GROUPS // gps
    reverse = direction == 1
    cidx = (lambda c: nc - 1 - c) if reverse else (lambda c: c)
    hrow = direction * nsteps
    nblk = gps * PAIRS_PER_GROUP
    nh = gps * HEADS_PER_GROUP
    return pl.pallas_call(
        functools.partial(_ssd_kernel, reverse=reverse),
        grid=(nsteps, nc),
        in_specs=[pl.BlockSpec((nblk, L, V7X_LANES), lambda g, c: (g, cidx(c), 0)),
                  pl.BlockSpec((gps, L, V7X_LANES), lambda g, c: (X_BLOCKS // gps + g, cidx(c), 0)),
                  pl.BlockSpec((gps, L, V7X_LANES), lambda g, c: ((X_BLOCKS + SSM_GROUPS) // gps + g, cidx(c), 0)),
                  pl.BlockSpec((nh, L), lambda g, c: (hrow + g, cidx(c))),
                  pl.BlockSpec((nh, 1), lambda g, c: (hrow + g, 0))],
        out_specs=pl.BlockSpec((nblk, L, V7X_LANES), lambda g, c: (g, cidx(c), 0)),
        out_shape=jax.ShapeDtypeStruct((X_BLOCKS, s, V7X_LANES), BF16),
        scratch_shapes=[pltpu.VMEM((SSM_STATE, nh * SSM_HEAD_DIM), F32)],
        compiler_params=_cparams(("parallel", "arbitrary"), 40),
        name="ssd_bwd" if reverse else "ssd_fwd",
    )(xbc_t, xbc_t, xbc_t, dtt, alog_col)


def _ssm_gate_norm(g, yf_ref, yb_ref, xs_ref, z_ref, d_ref, ng_ref):
    per_group = PAIRS_PER_GROUP
    gw = SSM_INNER // SSM_GROUPS
    ys = []
    ss = None
    for p in range(per_group):
        j = g * per_group + p
        lanes = slice(j * V7X_LANES, (j + 1) * V7X_LANES)
        y = (yf_ref[j].astype(F32) + yb_ref[j].astype(F32)
             + d_ref[:, lanes] * xs_ref[j].astype(F32))
        y = y * _silu(z_ref[:, lanes].astype(F32))
        ys.append(y)
        t = jnp.sum(y * y, axis=-1, keepdims=True)
        ss = t if ss is None else ss + t
    scale = lax.rsqrt(ss * (1.0 / gw) + EPS)
    pieces = []
    for p in range(per_group):
        j = g * per_group + p
        lanes = slice(j * V7X_LANES, (j + 1) * V7X_LANES)
        pieces.append((ys[p] * scale * ng_ref[:, lanes]).astype(BF16))
    return jnp.concatenate(pieces, axis=1)


def _ssm_out_kernel(x_ref, yf_ref, yb_ref, xs_ref, z_ref, d_ref, ng_ref, w_ref, gate_ref, o_ref):
    yn = jnp.concatenate([_ssm_gate_norm(g, yf_ref, yb_ref, xs_ref, z_ref, d_ref, ng_ref)
                          for g in range(SSM_GROUPS)], axis=1)
    o_ref[...] = x_ref[...] + gate_ref[...] * jnp.dot(yn, w_ref[...], preferred_element_type=F32)


def _ssm_out(x2, yf, yb, xbc_t, z, d_full, ng, w, gate):
    s, d = x2.shape
    tm = min(s, 256)
    yspec = pl.BlockSpec((X_BLOCKS, tm, V7X_LANES), lambda i: (0, i, 0))
    return pl.pallas_call(
        _ssm_out_kernel,
        grid=(s // tm,),
        in_specs=[pl.BlockSpec((tm, d), lambda i: (i, 0)), yspec, yspec, yspec,
                  pl.BlockSpec((tm, SSM_INNER), lambda i: (i, 0)),
                  pl.BlockSpec((1, SSM_INNER), lambda i: (0, 0)),
                  pl.BlockSpec((1, SSM_INNER), lambda i: (0, 0)),
                  _const_spec(w.shape),
                  pl.BlockSpec((1, d), lambda i: (0, 0))],
        out_specs=pl.BlockSpec((tm, d), lambda i: (i, 0)),
        out_shape=jax.ShapeDtypeStruct((s, d), F32),
        compiler_params=_cparams(("parallel",), 56),
        name="ssm_out",
    )(x2, yf, yb, xbc_t, z, d_full, ng, w, gate)


def _rot_cols(w):
    half = MLA_ROPE // 2
    return jnp.concatenate([-w[..., half:], w[..., :half]], axis=-1)


def _pad_lanes(w, width):
    return jnp.pad(w, [(0, 0)] * (w.ndim - 1) + [(0, width - w.shape[-1])])


def kernel(x, c, positions, ada_w, ada_b, norm_mix_g, norm_ffn_g, ffn_w1, ffn_w2, hyb_w_in, sgu_norm_g, sgu_w, sgu_b, mla_q_norm_g, mla_kv_norm_g, mla_w_uq, mla_w_ukv, hyb_w_out, ssm_w_in, ssm_conv_w, ssm_conv_b, ssm_dt_bias, ssm_a_log, ssm_d, ssm_norm_g, ssm_w_out, final_norm_g):
    batch, s, d = x.shape
    assert batch == 1 and d == D_MODEL and s % 1024 == 0
    depth = ada_w.shape[0]
    assert depth == 2
    x2 = x.reshape(s, d)

    mod = _ada_mod(c, ada_w, ada_b)
    mods = [[mod[l, :, i * d:(i + 1) * d] for i in range(6)] for l in range(depth)]
    row = lambda v: v.reshape(1, -1)

    sh1, sc1, g1, sh2, sc2, g2 = mods[0]
    w_in = hyb_w_in[0]
    c_pe = 2 * SGU_WIDTH + MLA_Q_RANK + MLA_KV_RANK
    w_kpe = w_in[:, c_pe:]
    win_ext = jnp.concatenate(
        [w_in[:, :c_pe], _pad_lanes(w_kpe, V7X_LANES), _pad_lanes(_rot_cols(w_kpe), V7X_LANES)],
        axis=1).astype(BF16)
    wq = mla_w_uq[0].reshape(MLA_Q_RANK, MLA_HEADS, MLA_NOPE + MLA_ROPE)
    wqa = _pad_lanes(wq, QK_PAD).reshape(MLA_Q_RANK, MLA_HEADS * QK_PAD).astype(BF16)
    wqb = _pad_lanes(_rot_cols(wq[..., MLA_NOPE:]), V7X_LANES).reshape(
        MLA_Q_RANK, MLA_HEADS * V7X_LANES).astype(BF16)
    wkv = mla_w_ukv[0].reshape(MLA_KV_RANK, MLA_HEADS, MLA_NOPE + MLA_V)
    wk = wkv[..., :MLA_NOPE].reshape(MLA_KV_RANK, MLA_HEADS * MLA_NOPE).astype(BF16)
    wv = wkv[..., MLA_NOPE:].reshape(MLA_KV_RANK, MLA_HEADS * MLA_V).astype(BF16)
    sb_full = jnp.repeat(sgu_b[0].T, SGU_HEAD, axis=1)

    half = MLA_ROPE // 2
    inv_freq = ROPE_THETA ** (-jnp.arange(half, dtype=F32) / half)
    inv128 = jnp.tile(inv_freq, V7X_LANES // half).reshape(1, V7X_LANES)
    cos, sin = _rope_tables(positions.astype(F32).reshape(s, 1), inv128)
    q_scale = (MLA_NOPE + MLA_ROPE) ** -0.5 * LOG2E

    a_out, q, k, v = _hyb_in(x2, row(norm_mix_g[0]), sc1, sh1, win_ext, row(sgu_norm_g[0]),
                             sgu_w[0].astype(BF16), sb_full, row(mla_q_norm_g[0]),
                             row(mla_kv_norm_g[0]), wqa, wqb, wk, wv, cos, sin, q_scale)
    o = _flash(q, k, v)
    x2 = _hyb_out(x2, a_out, o, hyb_w_out[0].astype(BF16), g1)
    w1_bf = ffn_w1.astype(BF16)
    w2_bf = ffn_w2.astype(BF16)
    x2 = _ffn(x2, row(norm_ffn_g[0]), sc2, sh2, g2, row(final_norm_g), w1_bf, w2_bf, 0, False)

    sh1, sc1, g1, sh2, sc2, g2 = mods[1]
    z, xbc_t, dtt = _ssm_in(x2, row(norm_mix_g[1]), sc1, sh1, ssm_w_in[0].astype(BF16),
                            ssm_dt_bias[0].reshape(1, 2 * SSM_HEADS),
                            ssm_conv_w[0], row(ssm_conv_b[0]))
    alog_col = ssm_a_log[0].reshape(2 * SSM_HEADS, 1)
    yf = _ssd_scan(xbc_t, dtt, alog_col, 0)
    yb = _ssd_scan(xbc_t, dtt, alog_col, 1)
    d_full = jnp.repeat(ssm_d[0], SSM_HEAD_DIM).reshape(1, SSM_INNER)
    x2 = _ssm_out(x2, yf, yb, xbc_t, z, d_full, row(ssm_norm_g[0]), ssm_w_out[0].astype(BF16), g1)
    x2 = _ffn(x2, row(norm_ffn_g[1]), sc2, sh2, g2, row(final_norm_g), w1_bf, w2_bf, 1, True)
    return x2.reshape(batch, s, d)
```

```python
import functools
import math

import jax
import jax.numpy as jnp
from jax import lax
from jax.experimental import pallas as pl
from jax.experimental.pallas import tpu as pltpu

F32 = jnp.float32
BF16 = jnp.bfloat16

V7X_LANES = 128
V7X_BF16_SUBLANE_TILE = 16
V7X_VMEM_BYTES = 64 * 1024 * 1024

D_MODEL = 2048
SGU_CHUNK = 128
SGU_GROUPS = 8
SGU_HEAD = 128
SGU_WIDTH = SGU_GROUPS * SGU_HEAD
MLA_HEADS = 8
MLA_Q_RANK = 512
MLA_KV_RANK = 512
MLA_NOPE = 128
MLA_ROPE = 64
MLA_V = 128
ROPE_THETA = 10000.0
HYB_MIX = SGU_WIDTH + MLA_HEADS * MLA_V
SSM_INNER = 2 * D_MODEL
SSM_HEAD_DIM = 64
SSM_HEADS = SSM_INNER // SSM_HEAD_DIM
SSM_GROUPS = 8
SSM_STATE = 128
SSM_CONV = 5
SSM_CHUNK = 256
SSM_CONV_CH = SSM_INNER + 2 * SSM_GROUPS * SSM_STATE
FFN_HIDDEN = 4 * D_MODEL
EPS = 1e-6

QK_PAD = 2 * V7X_LANES
HEADS_PER_GROUP = SSM_HEADS // SSM_GROUPS
PAIRS_PER_GROUP = HEADS_PER_GROUP * SSM_HEAD_DIM // V7X_LANES
X_BLOCKS = SSM_INNER // V7X_LANES
XBC_BLOCKS = SSM_CONV_CH // V7X_LANES
ZX_WIDTH = SSM_INNER + SSM_CONV_CH
LOG2E = 1.4426950408889634


def _cparams(sem, vmem_mib):
    return pltpu.CompilerParams(dimension_semantics=sem,
                                vmem_limit_bytes=vmem_mib * 1024 * 1024)


def _const_spec(shape):
    nd = len(shape)
    return pl.BlockSpec(shape, lambda *_: (0,) * nd, pipeline_mode=pl.Buffered(1))


def _norm_mod(x, g, sc, sh):
    y = x * lax.rsqrt(jnp.mean(x * x, axis=-1, keepdims=True) + EPS)
    return (y * g) * (1.0 + sc) + sh


def _silu(x):
    return x * jax.nn.sigmoid(x)


def _ada_kernel(c_ref, w_ref, b_ref, o_ref):
    c = c_ref[...]
    cond = jnp.broadcast_to(_silu(c), (8, c.shape[-1]))
    r = jnp.dot(cond, w_ref[0], preferred_element_type=F32,
                precision=lax.Precision.HIGHEST)
    o_ref[0] = r[0:1] + b_ref[0]


def _ada_mod(c, ada_w, ada_b):
    depth, d, n = ada_w.shape
    tn = 1024
    return pl.pallas_call(
        _ada_kernel,
        grid=(depth, n // tn),
        in_specs=[pl.BlockSpec((1, d), lambda l, j: (0, 0)),
                  pl.BlockSpec((1, d, tn), lambda l, j: (l, 0, j)),
                  pl.BlockSpec((1, 1, tn), lambda l, j: (l, 0, j))],
        out_specs=pl.BlockSpec((1, 1, tn), lambda l, j: (l, 0, j)),
        out_shape=jax.ShapeDtypeStruct((depth, 1, n), F32),
        compiler_params=_cparams(("parallel", "parallel"), 40),
        name="ada_mod",
    )(c, ada_w, ada_b.reshape(depth, 1, n))


def _rope_kernel(pos_ref, inv_ref, cos_ref, sin_ref):
    ang = pos_ref[...] * inv_ref[...]
    cos_ref[...] = jnp.cos(ang)
    sin_ref[...] = jnp.sin(ang)


def _rope_tables(pos_col, inv128):
    s = pos_col.shape[0]
    tm = min(s, 2048)
    return pl.pallas_call(
        _rope_kernel,
        grid=(s // tm,),
        in_specs=[pl.BlockSpec((tm, 1), lambda i: (i, 0)),
                  pl.BlockSpec((1, V7X_LANES), lambda i: (0, 0))],
        out_specs=[pl.BlockSpec((tm, V7X_LANES), lambda i: (i, 0))] * 2,
        out_shape=[jax.ShapeDtypeStruct((s, V7X_LANES), F32)] * 2,
        compiler_params=_cparams(("parallel",), 32),
        name="rope_tables",
    )(pos_col, inv128)


def _hyb_in_kernel(x_ref, g_ref, sc_ref, sh_ref, win_ref, lng_ref, sw_ref, sb_ref,
                   qg_ref, kvg_ref, wqa_ref, wqb_ref, wk_ref, wv_ref, cos_ref, sin_ref,
                   a_ref, q_ref, k_ref, v_ref, proj_scr, vn_scr, *, q_scale):
    tm = x_ref.shape[0]
    h = _norm_mod(x_ref[...], g_ref[...], sc_ref[...], sh_ref[...])
    proj_scr[...] = jnp.dot(h.astype(BF16), win_ref[...], preferred_element_type=F32)

    v = jax.nn.gelu(proj_scr[:, SGU_WIDTH:2 * SGU_WIDTH])
    vc = v - jnp.mean(v, axis=-1, keepdims=True)
    vn = vc * lax.rsqrt(jnp.mean(vc * vc, axis=-1, keepdims=True) + EPS) * lng_ref[...]
    vn_scr[...] = vn.astype(BF16)
    for c in range(tm // SGU_CHUNK):
        rows = slice(c * SGU_CHUNK, (c + 1) * SGU_CHUNK)
        for g in range(SGU_GROUPS):
            cols = slice(g * SGU_HEAD, (g + 1) * SGU_HEAD)
            mixed = jnp.dot(sw_ref[g], vn_scr[rows, cols], preferred_element_type=F32)
            u = jax.nn.gelu(proj_scr[rows, cols])
            a_ref[rows, cols] = (u * (mixed + sb_ref[:, cols])).astype(BF16)

    cos = cos_ref[...]
    sin = sin_ref[...]
    lat0 = 2 * SGU_WIDTH
    ql = proj_scr[:, lat0:lat0 + MLA_Q_RANK]
    qn = (ql * lax.rsqrt(jnp.mean(ql * ql, axis=-1, keepdims=True) + EPS) * qg_ref[...]).astype(BF16)
    qa = jnp.dot(qn, wqa_ref[...], preferred_element_type=F32)
    qb = jnp.dot(qn, wqb_ref[...], preferred_element_type=F32)
    for hd in range(MLA_HEADS):
        o = hd * QK_PAD
        q_ref[:, o:o + V7X_LANES] = (qa[:, o:o + V7X_LANES] * q_scale).astype(BF16)
        pe = (qa[:, o + V7X_LANES:o + QK_PAD] * cos
              + qb[:, hd * V7X_LANES:(hd + 1) * V7X_LANES] * sin)
        q_ref[:, o + V7X_LANES:o + QK_PAD] = (pe * q_scale).astype(BF16)

    kv0 = lat0 + MLA_Q_RANK
    kvl = proj_scr[:, kv0:kv0 + MLA_KV_RANK]
    kvn = (kvl * lax.rsqrt(jnp.mean(kvl * kvl, axis=-1, keepdims=True) + EPS) * kvg_ref[...]).astype(BF16)
    kn = jnp.dot(kvn, wk_ref[...], preferred_element_type=F32)
    vv = jnp.dot(kvn, wv_ref[...], preferred_element_type=F32)
    pe0 = kv0 + MLA_KV_RANK
    kpe = (proj_scr[:, pe0:pe0 + V7X_LANES] * cos
           + proj_scr[:, pe0 + V7X_LANES:pe0 + 2 * V7X_LANES] * sin).astype(BF16)
    ones = jnp.ones((tm, V7X_LANES), BF16)
    for hd in range(MLA_HEADS):
        o = hd * QK_PAD
        hs = slice(hd * V7X_LANES, (hd + 1) * V7X_LANES)
        k_ref[:, o:o + V7X_LANES] = kn[:, hs].astype(BF16)
        k_ref[:, o + V7X_LANES:o + QK_PAD] = kpe
        v_ref[:, o:o + V7X_LANES] = vv[:, hs].astype(BF16)
        v_ref[:, o + V7X_LANES:o + QK_PAD] = ones


def _hyb_in(x2, g, sc, sh, win, lng, sw, sb, qg, kvg, wqa, wqb, wk, wv, cos, sin, q_scale):
    s, d = x2.shape
    tm = min(s, 512)
    nproj = win.shape[1]
    row = lambda w: pl.BlockSpec((tm, w), lambda i: (i, 0))
    vec = lambda w: pl.BlockSpec((1, w), lambda i: (0, 0))
    hq = MLA_HEADS * QK_PAD
    return pl.pallas_call(
        functools.partial(_hyb_in_kernel, q_scale=q_scale),
        grid=(s // tm,),
        in_specs=[row(d), vec(d), vec(d), vec(d), _const_spec(win.shape), vec(SGU_WIDTH),
                  _const_spec(sw.shape), _const_spec(sb.shape), vec(MLA_Q_RANK), vec(MLA_KV_RANK),
                  _const_spec(wqa.shape), _const_spec(wqb.shape), _const_spec(wk.shape),
                  _const_spec(wv.shape), row(V7X_LANES), row(V7X_LANES)],
        out_specs=[row(SGU_WIDTH), row(hq), row(hq), row(hq)],
        out_shape=[jax.ShapeDtypeStruct((s, SGU_WIDTH), BF16),
                   jax.ShapeDtypeStruct((s, hq), BF16),
                   jax.ShapeDtypeStruct((s, hq), BF16),
                   jax.ShapeDtypeStruct((s, hq), BF16)],
        scratch_shapes=[pltpu.VMEM((tm, nproj), F32), pltpu.VMEM((tm, SGU_WIDTH), BF16)],
        compiler_params=_cparams(("parallel",), 56),
        name="hyb_in",
    )(x2, g, sc, sh, win, lng, sw, sb, qg, kvg, wqa, wqb, wk, wv, cos, sin)


def _flash_kernel(q_ref, k_ref, v_ref, o_ref, m_scr, acc_scr, *, tk, sub):
    nk = k_ref.shape[0] // tk
    tq = q_ref.shape[0]
    m_scr[...] = jnp.full(m_scr.shape, -jnp.inf, F32)
    acc_scr[...] = jnp.zeros(acc_scr.shape, F32)

    def body(j, carry):
        off = pl.multiple_of(j * tk, tk)
        k = k_ref[pl.ds(off, tk), :]
        v = v_ref[pl.ds(off, tk), :]
        for r in range(tq // sub):
            rows = slice(r * sub, (r + 1) * sub)
            s = lax.dot_general(q_ref[rows, :], k, (((1,), (1,)), ((), ())),
                                preferred_element_type=F32)
            m_prev = m_scr[rows, :]
            m_new = jnp.maximum(m_prev, jnp.max(s, axis=-1, keepdims=True))
            alpha = jnp.exp2(m_prev - m_new)
            p = jnp.exp2(s - jnp.tile(m_new, (1, tk // V7X_LANES)))
            acc_scr[rows, :] = (acc_scr[rows, :] * jnp.tile(alpha, (1, 2))
                                + jnp.dot(p.astype(BF16), v, preferred_element_type=F32))
            m_scr[rows, :] = m_new
        return carry

    lax.fori_loop(0, nk, body, 0, unroll=8 if nk % 8 == 0 else 1)
    acc = acc_scr[...]
    o_ref[...] = (acc[:, :MLA_V] / acc[:, MLA_V:]).astype(BF16)


def _flash(q, k, v):
    s = q.shape[0]
    tq = min(s, 2048)
    tk = min(s, 1024)
    return pl.pallas_call(
        functools.partial(_flash_kernel, tk=tk, sub=min(tq, 256)),
        grid=(MLA_HEADS, s // tq),
        in_specs=[pl.BlockSpec((tq, QK_PAD), lambda h, i: (i, h)),
                  pl.BlockSpec((s, QK_PAD), lambda h, i: (0, h)),
                  pl.BlockSpec((s, QK_PAD), lambda h, i: (0, h))],
        out_specs=pl.BlockSpec((tq, MLA_V), lambda h, i: (i, h)),
        out_shape=jax.ShapeDtypeStruct((s, MLA_HEADS * MLA_V), BF16),
        scratch_shapes=[pltpu.VMEM((tq, V7X_LANES), F32), pltpu.VMEM((tq, QK_PAD), F32)],
        compiler_params=_cparams(("parallel", "arbitrary"), 56),
        name="flash_attn",
    )(q, k, v)


def _hyb_out_kernel(x_ref, a_ref, o_ref, w_ref, gate_ref, y_ref):
    m = (jnp.dot(a_ref[...], w_ref[:SGU_WIDTH, :], preferred_element_type=F32)
         + jnp.dot(o_ref[...], w_ref[SGU_WIDTH:, :], preferred_element_type=F32))
    y_ref[...] = x_ref[...] + gate_ref[...] * m


def _hyb_out(x2, a, o, w, gate):
    s, d = x2.shape
    tm = min(s, 512)
    return pl.pallas_call(
        _hyb_out_kernel,
        grid=(s // tm,),
        in_specs=[pl.BlockSpec((tm, d), lambda i: (i, 0)),
                  pl.BlockSpec((tm, a.shape[1]), lambda i: (i, 0)),
                  pl.BlockSpec((tm, o.shape[1]), lambda i: (i, 0)),
                  _const_spec(w.shape),
                  pl.BlockSpec((1, d), lambda i: (0, 0))],
        out_specs=pl.BlockSpec((tm, d), lambda i: (i, 0)),
        out_shape=jax.ShapeDtypeStruct((s, d), F32),
        compiler_params=_cparams(("parallel",), 48),
        name="hyb_out",
    )(x2, a, o, w, gate)


def _ffn_kernel(x_ref, g_ref, sc_ref, sh_ref, gate_ref, fg_ref, w1_ref, w2_ref, o_ref,
                h_scr, acc_scr, *, final_norm):
    k = pl.program_id(1)

    @pl.when(k == 0)
    def _():
        h_scr[...] = _norm_mod(x_ref[...], g_ref[...], sc_ref[...], sh_ref[...]).astype(BF16)
        acc_scr[...] = jnp.zeros(acc_scr.shape, F32)

    hid = jnp.dot(h_scr[...], w1_ref[...], preferred_element_type=F32)
    hid = jnp.square(jnp.maximum(hid, 0.0)).astype(BF16)
    acc_scr[...] += jnp.dot(hid, w2_ref[...], preferred_element_type=F32)

    @pl.when(k == pl.num_programs(1) - 1)
    def _():
        y = x_ref[...] + gate_ref[...] * acc_scr[...]
        if final_norm:
            y = y * lax.rsqrt(jnp.mean(y * y, axis=-1, keepdims=True) + EPS) * fg_ref[...]
        o_ref[...] = y


def _ffn(x2, g, sc, sh, gate, fg, w1, w2, layer, final_norm):
    s, d = x2.shape
    hidden = w1.shape[2]
    tm = min(s, 512)
    tk = 1024
    vec = pl.BlockSpec((1, d), lambda i, k: (0, 0))
    return pl.pallas_call(
        functools.partial(_ffn_kernel, final_norm=final_norm),
        grid=(s // tm, hidden // tk),
        in_specs=[pl.BlockSpec((tm, d), lambda i, k: (i, 0)), vec, vec, vec, vec, vec,
                  pl.BlockSpec((None, d, tk), lambda i, k: (layer, 0, k)),
                  pl.BlockSpec((None, tk, d), lambda i, k: (layer, k, 0))],
        out_specs=pl.BlockSpec((tm, d), lambda i, k: (i, 0)),
        out_shape=jax.ShapeDtypeStruct((s, d), F32),
        scratch_shapes=[pltpu.VMEM((tm, d), BF16), pltpu.VMEM((tm, d), F32)],
        compiler_params=_cparams(("parallel", "arbitrary"), 48),
        name="ffn",
    )(x2, g, sc, sh, gate, fg, w1, w2)


SSM_IN_HALO = V7X_BF16_SUBLANE_TILE
SSM_IN_SUB = 256
SSM_IN_ROWS = 1024


def _ssm_in_kernel(xp_ref, x_ref, xn_ref, g_ref, sc_ref, sh_ref, w_ref, wdt_ref, dtb_ref,
                   cw_ref, cb_ref, z_ref, xbc_ref, dtt_ref, h_scr, *, nz):
    i = pl.program_id(0)
    j = pl.program_id(1)
    tm = x_ref.shape[0]
    halo = xp_ref.shape[0]
    pad = SSM_CONV // 2

    @pl.when(j == 0)
    def _():
        norm = lambda v: _norm_mod(v, g_ref[...], sc_ref[...], sh_ref[...])
        hb = norm(x_ref[...]).astype(BF16)
        h_scr[0:halo, :] = jnp.where(i == 0, 0.0, norm(xp_ref[...])).astype(BF16)
        h_scr[halo:halo + tm, :] = hb
        h_scr[halo + tm:, :] = jnp.where(i == pl.num_programs(0) - 1, 0.0, norm(xn_ref[...])).astype(BF16)
        r = jnp.dot(hb, wdt_ref[...], preferred_element_type=F32) + dtb_ref[...]
        dtt_ref[...] = (jnp.maximum(r, 0.0) + jnp.log1p(jnp.exp(-jnp.abs(r)))).T

    @pl.when(j < nz)
    def _():
        z_ref[...] = jnp.dot(h_scr[halo:halo + tm, :], w_ref[...],
                             preferred_element_type=F32).astype(BF16)

    @pl.when(j >= nz)
    def _():
        rb = min(tm, SSM_IN_ROWS)
        for r in range(tm // rb):
            for c in range(w_ref.shape[1] // SSM_IN_SUB):
                cols = slice(c * SSM_IN_SUB, (c + 1) * SSM_IN_SUB)
                pr = jnp.dot(h_scr[r * rb:r * rb + rb + 2 * halo, :], w_ref[:, cols],
                             preferred_element_type=F32)
                acc = jnp.broadcast_to(cb_ref[:, cols], (rb, SSM_IN_SUB))
                for t in range(SSM_CONV):
                    acc = acc + cw_ref[t:t + 1, cols] * pr[halo - pad + t:halo - pad + t + rb, :]
                y = _silu(acc)
                for u in range(SSM_IN_SUB // V7X_LANES):
                    xbc_ref[c * (SSM_IN_SUB // V7X_LANES) + u, r * rb:(r + 1) * rb, :] = (
                        y[:, u * V7X_LANES:(u + 1) * V7X_LANES].astype(BF16))


def _ssm_in(x2, g, sc, sh, w_in, dt_bias, conv_w, conv_b):
    s, d = x2.shape
    ndt = 2 * SSM_HEADS
    tm = min(s, 1024)
    tn = 1024
    halo = SSM_IN_HALO
    nz = SSM_INNER // tn
    nrow_h = s // halo
    vec = pl.BlockSpec((1, d), lambda i, j: (0, 0))
    ccol = lambda j: jnp.maximum(j - nz, 0)
    return pl.pallas_call(
        functools.partial(_ssm_in_kernel, nz=nz),
        grid=(s // tm, ZX_WIDTH // tn),
        in_specs=[pl.BlockSpec((halo, d), lambda i, j: (jnp.maximum(i * (tm // halo) - 1, 0), 0)),
                  pl.BlockSpec((tm, d), lambda i, j: (i, 0)),
                  pl.BlockSpec((halo, d), lambda i, j: (jnp.minimum((i + 1) * (tm // halo), nrow_h - 1), 0)),
                  vec, vec, vec,
                  pl.BlockSpec((d, tn), lambda i, j: (0, j)),
                  pl.BlockSpec((d, ndt), lambda i, j: (0, ZX_WIDTH // ndt)),
                  pl.BlockSpec((1, ndt), lambda i, j: (0, 0)),
                  pl.BlockSpec((SSM_CONV, tn), lambda i, j: (0, ccol(j))),
                  pl.BlockSpec((1, tn), lambda i, j: (0, ccol(j)))],
        out_specs=[pl.BlockSpec((tm, tn), lambda i, j: (i, jnp.minimum(j, nz - 1))),
                   pl.BlockSpec((tn // V7X_LANES, tm, V7X_LANES), lambda i, j: (ccol(j), i, 0)),
                   pl.BlockSpec((ndt, tm), lambda i, j: (0, i))],
        out_shape=[jax.ShapeDtypeStruct((s, SSM_INNER), BF16),
                   jax.ShapeDtypeStruct((XBC_BLOCKS, s, V7X_LANES), BF16),
                   jax.ShapeDtypeStruct((ndt, s), F32)],
        scratch_shapes=[pltpu.VMEM((tm + 2 * halo, d), BF16)],
        compiler_params=_cparams(("parallel", "arbitrary"), 48),
        name="ssm_in",
    )(x2, x2, x2, g, sc, sh, w_in, w_in, dt_bias, conv_w, conv_b)


def _chunk_cumsum(dta, tri):
    n = dta.shape[0]
    hi = dta.astype(BF16)
    r1 = dta - hi.astype(F32)
    mid = r1.astype(BF16)
    lo = (r1 - mid.astype(F32)).astype(BF16)
    cs = jnp.dot(jnp.concatenate([hi, mid, lo], axis=0), tri, preferred_element_type=F32)
    return cs[0:n] + cs[n:2 * n] + cs[2 * n:3 * n]


def _lane_bcast_col(row, L):
    return jnp.broadcast_to(row, (V7X_LANES, L)).T


def _ssd_kernel(xa_ref, ba_ref, ca_ref, dta_ref, xb_ref, bb_ref, cb_ref, dtb_ref, alog_ref,
                ya_ref, yb_ref, stf_scr, stb_scr):
    L = xa_ref.shape[1]
    P = SSM_HEAD_DIM
    H = SSM_HEADS

    @pl.when(pl.program_id(0) == 0)
    def _():
        stf_scr[...] = jnp.zeros(stf_scr.shape, F32)
        stb_scr[...] = jnp.zeros(stb_scr.shape, F32)

    neg_a = -jnp.exp(alog_ref[...])
    dt_a = dta_ref[...]
    dt_b = dtb_ref[H:, :]
    kk = lax.broadcasted_iota(jnp.int32, (L, L), 0)
    ii = lax.broadcasted_iota(jnp.int32, (L, L), 1)
    tri_f = jnp.where(kk <= ii, 1.0, 0.0).astype(BF16)
    tri_b = jnp.where(kk >= ii, 1.0, 0.0).astype(BF16)
    cf = _chunk_cumsum(dt_a[:H] * neg_a[:H], tri_f) * LOG2E
    rc = _chunk_cumsum(jnp.concatenate([dt_a[H:] * neg_a[H:], dt_b * neg_a[H:]], axis=0), tri_b) * LOG2E
    ra = rc[:H]
    rb = rc[H:]
    cf_dt = cf - jnp.log(dt_a[:H]) * LOG2E
    ra_dt = ra - jnp.log(dt_a[H:]) * LOG2E
    rb_dt = rb - jnp.log(dt_b) * LOG2E
    f_end = cf[:, L - 1:L]
    b_end = rb[:, 0:1]
    to_end_f = jnp.exp2(f_end - cf_dt)
    to_end_b = jnp.exp2(b_end - rb_dt)
    dec_f = jnp.exp2(f_end)
    dec_b = jnp.exp2(b_end)
    below = kk > ii
    diag = kk == ii
    dt_sum = jnp.log(dt_a[:H] + dt_a[H:]) * LOG2E
    low =lax.broadcasted_iota(jnp.int32, (1, V7X_LANES), 1) < P
    heads_per_block = V7X_LANES // P
    reps = (1, L // V7X_LANES)

    for g in range(SSM_GROUPS):
        gl = slice(g * HEADS_PER_GROUP * P, (g + 1) * HEADS_PER_GROUP * P)
        cga = ca_ref[g]
        bta = ba_ref[g].astype(F32).T.astype(BF16)
        cba = jnp.dot(cga, bta, preferred_element_type=F32).astype(BF16)
        stf = stf_scr[:, gl]
        ysf = jnp.dot(cga, stf.astype(BF16), preferred_element_type=F32)
        cgb = cb_ref[g]
        btb = bb_ref[g].astype(F32).T.astype(BF16)
        stb = stb_scr[:, gl]
        ysb = jnp.dot(cgb, stb.astype(BF16), preferred_element_type=F32)
        for q in range(PAIRS_PER_GROUP):
            blk = g * PAIRS_PER_GROUP + q
            xpa = xa_ref[blk]
            xpb = xb_ref[blk]
            lanes = slice(q * V7X_LANES, (q + 1) * V7X_LANES)
            yas, ybs, sfs, sbs = [], [], [], []
            for e in range(heads_per_block):
                hd = blk * heads_per_block + e
                hr = slice(hd, hd + 1)
                cf_col = _lane_bcast_col(cf[hr, :], L)
                ra_col = _lane_bcast_col(ra[hr, :], L)
                rb_col = _lane_bcast_col(rb[hr, :], L)
                arg = jnp.where(below, jnp.tile(cf_col, reps) - cf_dt[hr, :],
                                jnp.where(diag, dt_sum[hr, :], jnp.tile(ra_col, reps) - ra_dt[hr, :]))
                w = jnp.exp2(arg).astype(BF16) * cba
                yh = jnp.dot(w, xpa, preferred_element_type=F32)
                yas.append(yh + ysf[:, lanes] * jnp.exp2(cf_col))
                ybs.append(ysb[:, lanes] * jnp.exp2(rb_col))
                sfs.append(jnp.dot(bta * to_end_f[hr, :].astype(BF16), xpa, preferred_element_type=F32))
                sbs.append(jnp.dot(btb * to_end_b[hr, :].astype(BF16), xpb, preferred_element_type=F32))
            ya_ref[blk] = jnp.where(low, yas[0], yas[1]).astype(BF16)
            yb_ref[blk] = jnp.where(low, ybs[0], ybs[1]).astype(BF16)
            h0 = blk * heads_per_block
            sl = slice(blk * V7X_LANES, (blk + 1) * V7X_LANES)
            decf = jnp.where(low, dec_f[h0:h0 + 1, :], dec_f[h0 + 1:h0 + 2, :])
            decb = jnp.where(low, dec_b[h0:h0 + 1, :], dec_b[h0 + 1:h0 + 2, :])
            stf_scr[:, sl] = stf[:, lanes] * decf + jnp.where(low, sfs[0], sfs[1])
            stb_scr[:, sl] = stb[:, lanes] * decb + jnp.where(low, sbs[0], sbs[1])


def _ssd_scan(xbc_t, dtt, alog_col):
    s = xbc_t.shape[1]
    L = SSM_CHUNK
    nc = s // L
    fwd = lambda t: t
    bwd = lambda t: nc - 1 - t
    b0 = X_BLOCKS // SSM_GROUPS
    c0 = (X_BLOCKS + SSM_GROUPS) // SSM_GROUPS

    def chunk_specs(cidx):
        return [pl.BlockSpec((X_BLOCKS, L, V7X_LANES), lambda t: (0, cidx(t), 0)),
                pl.BlockSpec((SSM_GROUPS, L, V7X_LANES), lambda t: (b0, cidx(t), 0)),
                pl.BlockSpec((SSM_GROUPS, L, V7X_LANES), lambda t: (c0, cidx(t), 0)),
                pl.BlockSpec((2 * SSM_HEADS, L), lambda t: (0, cidx(t)))]

    yspec = lambda cidx: pl.BlockSpec((X_BLOCKS, L, V7X_LANES), lambda t: (0, cidx(t), 0))
    yshape = jax.ShapeDtypeStruct((X_BLOCKS, s, V7X_LANES), BF16)
    return pl.pallas_call(
        _ssd_kernel,
        grid=(nc,),
        in_specs=chunk_specs(fwd) + chunk_specs(bwd)
        + [pl.BlockSpec((2 * SSM_HEADS, 1), lambda t: (0, 0))],
        out_specs=[yspec(fwd), yspec(bwd)],
        out_shape=[yshape, yshape],
        scratch_shapes=[pltpu.VMEM((SSM_STATE, SSM_INNER), F32), pltpu.VMEM((SSM_STATE, SSM_INNER), F32)],
        compiler_params=_cparams(("arbitrary",), 48),
        name="ssd_bidir",
    )(xbc_t, xbc_t, xbc_t, dtt, xbc_t, xbc_t, xbc_t, dtt, alog_col)


def _ssm_gate_norm(g, yf_ref, yb_ref, xs_ref, z_ref, d_ref, ng_ref):
    per_group = PAIRS_PER_GROUP
    gw = SSM_INNER // SSM_GROUPS
    ys = []
    ss = None
    for p in range(per_group):
        j = g * per_group + p
        lanes = slice(j * V7X_LANES, (j + 1) * V7X_LANES)
        y = (yf_ref[j].astype(F32) + yb_ref[j].astype(F32)
             + d_ref[:, lanes] * xs_ref[j].astype(F32))
        y = y * _silu(z_ref[:, lanes].astype(F32))
        ys.append(y)
        t = jnp.sum(y * y, axis=-1, keepdims=True)
        ss = t if ss is None else ss + t
    scale = lax.rsqrt(ss * (1.0 / gw) + EPS)
    pieces = []
    for p in range(per_group):
        j = g * per_group + p
        lanes = slice(j * V7X_LANES, (j + 1) * V7X_LANES)
        pieces.append((ys[p] * scale * ng_ref[:, lanes]).astype(BF16))
    return jnp.concatenate(pieces, axis=1)


def _ssm_out_kernel(x_ref, yf_ref, yb_ref, xs_ref, z_ref, d_ref, ng_ref, w_ref, gate_ref, o_ref):
    yn = jnp.concatenate([_ssm_gate_norm(g, yf_ref, yb_ref, xs_ref, z_ref, d_ref, ng_ref)
                          for g in range(SSM_GROUPS)], axis=1)
    o_ref[...] = x_ref[...] + gate_ref[...] * jnp.dot(yn, w_ref[...], preferred_element_type=F32)


def _ssm_out(x2, yf, yb, xbc_t, z, d_full, ng, w, gate):
    s, d = x2.shape
    tm = min(s, 256)
    yspec = pl.BlockSpec((X_BLOCKS, tm, V7X_LANES), lambda i: (0, i, 0))
    return pl.pallas_call(
        _ssm_out_kernel,
        grid=(s // tm,),
        in_specs=[pl.BlockSpec((tm, d), lambda i: (i, 0)), yspec, yspec, yspec,
                  pl.BlockSpec((tm, SSM_INNER), lambda i: (i, 0)),
                  pl.BlockSpec((1, SSM_INNER), lambda i: (0, 0)),
                  pl.BlockSpec((1, SSM_INNER), lambda i: (0, 0)),
                  _const_spec(w.shape),
                  pl.BlockSpec((1, d), lambda i: (0, 0))],
        out_specs=pl.BlockSpec((tm, d), lambda i: (i, 0)),
        out_shape=jax.ShapeDtypeStruct((s, d), F32),
        compiler_params=_cparams(("parallel",), 56),
        name="ssm_out",
    )(x2, yf, yb, xbc_t, z, d_full, ng, w, gate)


def _rot_cols(w):
    half = MLA_ROPE // 2
    return jnp.concatenate([-w[..., half:], w[..., :half]], axis=-1)


def _pad_lanes(w, width):
    return jnp.pad(w, [(0, 0)] * (w.ndim - 1) + [(0, width - w.shape[-1])])


def kernel(x, c, positions, ada_w, ada_b, norm_mix_g, norm_ffn_g, ffn_w1, ffn_w2, hyb_w_in, sgu_norm_g, sgu_w, sgu_b, mla_q_norm_g, mla_kv_norm_g, mla_w_uq, mla_w_ukv, hyb_w_out, ssm_w_in, ssm_conv_w, ssm_conv_b, ssm_dt_bias, ssm_a_log, ssm_d, ssm_norm_g, ssm_w_out, final_norm_g):
    batch, s, d = x.shape
    assert batch == 1 and d == D_MODEL and s % 1024 == 0
    depth = ada_w.shape[0]
    assert depth == 2
    x2 = x.reshape(s, d)

    mod = _ada_mod(c, ada_w, ada_b)
    mods = [[mod[l, :, i * d:(i + 1) * d] for i in range(6)] for l in range(depth)]
    row = lambda v: v.reshape(1, -1)

    sh1, sc1, g1, sh2, sc2, g2 = mods[0]
    w_in = hyb_w_in[0]
    c_pe = 2 * SGU_WIDTH + MLA_Q_RANK + MLA_KV_RANK
    w_kpe = w_in[:, c_pe:]
    win_ext = jnp.concatenate(
        [w_in[:, :c_pe], _pad_lanes(w_kpe, V7X_LANES), _pad_lanes(_rot_cols(w_kpe), V7X_LANES)],
        axis=1).astype(BF16)
    wq = mla_w_uq[0].reshape(MLA_Q_RANK, MLA_HEADS, MLA_NOPE + MLA_ROPE)
    wqa = _pad_lanes(wq, QK_PAD).reshape(MLA_Q_RANK, MLA_HEADS * QK_PAD).astype(BF16)
    wqb = _pad_lanes(_rot_cols(wq[..., MLA_NOPE:]), V7X_LANES).reshape(
        MLA_Q_RANK, MLA_HEADS * V7X_LANES).astype(BF16)
    wkv = mla_w_ukv[0].reshape(MLA_KV_RANK, MLA_HEADS, MLA_NOPE + MLA_V)
    wk = wkv[..., :MLA_NOPE].reshape(MLA_KV_RANK, MLA_HEADS * MLA_NOPE).astype(BF16)
    wv = wkv[..., MLA_NOPE:].reshape(MLA_KV_RANK, MLA_HEADS * MLA_V).astype(BF16)
    sb_full = jnp.repeat(sgu_b[0].T, SGU_HEAD, axis=1)

    half = MLA_ROPE // 2
    inv_freq = ROPE_THETA ** (-jnp.arange(half, dtype=F32) / half)
    inv128 = jnp.tile(inv_freq, V7X_LANES // half).reshape(1, V7X_LANES)
    cos, sin = _rope_tables(positions.astype(F32).reshape(s, 1), inv128)
    q_scale = (MLA_NOPE + MLA_ROPE) ** -0.5 * LOG2E

    a_out, q, k, v = _hyb_in(x2, row(norm_mix_g[0]), sc1, sh1, win_ext, row(sgu_norm_g[0]),
                             sgu_w[0].astype(BF16), sb_full, row(mla_q_norm_g[0]),
                             row(mla_kv_norm_g[0]), wqa, wqb, wk, wv, cos, sin, q_scale)
    o = _flash(q, k, v)
    x2 = _hyb_out(x2, a_out, o, hyb_w_out[0].astype(BF16), g1)
    w1_bf = ffn_w1.astype(BF16)
    w2_bf = ffn_w2.astype(BF16)
    x2 = _ffn(x2, row(norm_ffn_g[0]), sc2, sh2, g2, row(final_norm_g), w1_bf, w2_bf, 0, False)

    sh1, sc1, g1, sh2, sc2, g2 = mods[1]
    z, xbc_t, dtt = _ssm_in(x2, row(norm_mix_g[1]), sc1, sh1, ssm_w_in[0].astype(BF16),
                            ssm_dt_bias[0].reshape(1, 2 * SSM_HEADS),
                            ssm_conv_w[0], row(ssm_conv_b[0]))
    alog_col = ssm_a_log[0].reshape(2 * SSM_HEADS, 1)
    yf, yb = _ssd_scan(xbc_t, dtt, alog_col)
    d_full = jnp.repeat(ssm_d[0], SSM_HEAD_DIM).reshape(1, SSM_INNER)
    x2 = _ssm_out(x2, yf, yb, xbc_t, z, d_full, row(ssm_norm_g[0]), ssm_w_out[0].astype(BF16), g1)
    x2 = _ffn(x2, row(norm_ffn_g[1]), sc2, sh2, g2, row(final_norm_g), w1_bf, w2_bf, 1, True)
    return x2.reshape(batch, s, d)
```

```python
import functools
import math

import jax
import jax.numpy as jnp
from jax import lax
from jax.experimental import pallas as pl
from jax.experimental.pallas import tpu as pltpu

F32 = jnp.float32
BF16 = jnp.bfloat16

V7X_LANES = 128
V7X_BF16_SUBLANE_TILE = 16
V7X_VMEM_BYTES = 64 * 1024 * 1024

D_MODEL = 2048
SGU_CHUNK = 128
SGU_GROUPS = 8
SGU_HEAD = 128
SGU_WIDTH = SGU_GROUPS * SGU_HEAD
MLA_HEADS = 8
MLA_Q_RANK = 512
MLA_KV_RANK = 512
MLA_NOPE = 128
MLA_ROPE = 64
MLA_V = 128
ROPE_THETA = 10000.0
HYB_MIX = SGU_WIDTH + MLA_HEADS * MLA_V
SSM_INNER = 2 * D_MODEL
SSM_HEAD_DIM = 64
SSM_HEADS = SSM_INNER // SSM_HEAD_DIM
SSM_GROUPS = 8
SSM_STATE = 128
SSM_CONV = 5
SSM_CHUNK = 256
SSM_CONV_CH = SSM_INNER + 2 * SSM_GROUPS * SSM_STATE
FFN_HIDDEN = 4 * D_MODEL
EPS = 1e-6

QK_PAD = 2 * V7X_LANES
HEADS_PER_GROUP = SSM_HEADS // SSM_GROUPS
PAIRS_PER_GROUP = HEADS_PER_GROUP * SSM_HEAD_DIM // V7X_LANES
X_BLOCKS = SSM_INNER // V7X_LANES
XBC_BLOCKS = SSM_CONV_CH // V7X_LANES
ZX_WIDTH = SSM_INNER + SSM_CONV_CH
LOG2E = 1.4426950408889634


def _cparams(sem, vmem_mib):
    return pltpu.CompilerParams(dimension_semantics=sem,
                                vmem_limit_bytes=vmem_mib * 1024 * 1024)


def _const_spec(shape):
    nd = len(shape)
    return pl.BlockSpec(shape, lambda *_: (0,) * nd, pipeline_mode=pl.Buffered(1))


def _norm_mod(x, g, sc, sh):
    y = x * lax.rsqrt(jnp.mean(x * x, axis=-1, keepdims=True) + EPS)
    return (y * g) * (1.0 + sc) + sh


def _silu(x):
    return x * jax.nn.sigmoid(x)


def _ada_kernel(c_ref, w_ref, b_ref, o_ref):
    c = c_ref[...]
    d = c.shape[-1]
    tn = w_ref.shape[2]
    cond_col = jnp.broadcast_to(_silu(c), (V7X_LANES, d)).T
    prod = w_ref[0] * jnp.tile(cond_col, (1, tn // V7X_LANES))
    o_ref[0] = jnp.sum(prod, axis=0, keepdims=True) + b_ref[0]


def _ada_mod(c, ada_w, ada_b):
    depth, d, n = ada_w.shape
    tn = 1024
    return pl.pallas_call(
        _ada_kernel,
        grid=(depth, n // tn),
        in_specs=[pl.BlockSpec((1, d), lambda l, j: (0, 0)),
                  pl.BlockSpec((1, d, tn), lambda l, j: (l, 0, j)),
                  pl.BlockSpec((1, 1, tn), lambda l, j: (l, 0, j))],
        out_specs=pl.BlockSpec((1, 1, tn), lambda l, j: (l, 0, j)),
        out_shape=jax.ShapeDtypeStruct((depth, 1, n), F32),
        compiler_params=_cparams(("parallel", "parallel"), 40),
        name="ada_mod",
    )(c, ada_w, ada_b.reshape(depth, 1, n))


def _rope_kernel(pos_ref, inv_ref, cos_ref, sin_ref):
    ang = pos_ref[...] * inv_ref[...]
    cos_ref[...] = jnp.cos(ang)
    sin_ref[...] = jnp.sin(ang)


def _rope_tables(pos_col, inv128):
    s = pos_col.shape[0]
    tm = min(s, 2048)
    return pl.pallas_call(
        _rope_kernel,
        grid=(s // tm,),
        in_specs=[pl.BlockSpec((tm, 1), lambda i: (i, 0)),
                  pl.BlockSpec((1, V7X_LANES), lambda i: (0, 0))],
        out_specs=[pl.BlockSpec((tm, V7X_LANES), lambda i: (i, 0))] * 2,
        out_shape=[jax.ShapeDtypeStruct((s, V7X_LANES), F32)] * 2,
        compiler_params=_cparams(("parallel",), 32),
        name="rope_tables",
    )(pos_col, inv128)


def _hyb_in_kernel(x_ref, g_ref, sc_ref, sh_ref, win_ref, lng_ref, sw_ref, sb_ref,
                   qg_ref, kvg_ref, wqa_ref, wqb_ref, wk_ref, wv_ref, cos_ref, sin_ref,
                   a_ref, q_ref, k_ref, v_ref, proj_scr, vn_scr, *, q_scale):
    tm = x_ref.shape[0]
    h = _norm_mod(x_ref[...], g_ref[...], sc_ref[...], sh_ref[...])
    proj_scr[...] = jnp.dot(h.astype(BF16), win_ref[...], preferred_element_type=F32)

    v = jax.nn.gelu(proj_scr[:, SGU_WIDTH:2 * SGU_WIDTH])
    vc = v - jnp.mean(v, axis=-1, keepdims=True)
    vn = vc * lax.rsqrt(jnp.mean(vc * vc, axis=-1, keepdims=True) + EPS) * lng_ref[...]
    vn_scr[...] = vn.astype(BF16)
    for c in range(tm // SGU_CHUNK):
        rows = slice(c * SGU_CHUNK, (c + 1) * SGU_CHUNK)
        for g in range(SGU_GROUPS):
            cols = slice(g * SGU_HEAD, (g + 1) * SGU_HEAD)
            mixed = jnp.dot(sw_ref[g], vn_scr[rows, cols], preferred_element_type=F32)
            u = jax.nn.gelu(proj_scr[rows, cols])
            a_ref[rows, cols] = (u * (mixed + sb_ref[:, cols])).astype(BF16)

    cos = cos_ref[...]
    sin = sin_ref[...]
    lat0 = 2 * SGU_WIDTH
    ql = proj_scr[:, lat0:lat0 + MLA_Q_RANK]
    qn = (ql * lax.rsqrt(jnp.mean(ql * ql, axis=-1, keepdims=True) + EPS) * qg_ref[...]).astype(BF16)
    qa = jnp.dot(qn, wqa_ref[...], preferred_element_type=F32)
    qb = jnp.dot(qn, wqb_ref[...], preferred_element_type=F32)
    for hd in range(MLA_HEADS):
        o = hd * QK_PAD
        q_ref[:, o:o + V7X_LANES] = (qa[:, o:o + V7X_LANES] * q_scale).astype(BF16)
        pe = (qa[:, o + V7X_LANES:o + QK_PAD] * cos
              + qb[:, hd * V7X_LANES:(hd + 1) * V7X_LANES] * sin)
        q_ref[:, o + V7X_LANES:o + QK_PAD] = (pe * q_scale).astype(BF16)

    kv0 = lat0 + MLA_Q_RANK
    kvl = proj_scr[:, kv0:kv0 + MLA_KV_RANK]
    kvn = (kvl * lax.rsqrt(jnp.mean(kvl * kvl, axis=-1, keepdims=True) + EPS) * kvg_ref[...]).astype(BF16)
    kn = jnp.dot(kvn, wk_ref[...], preferred_element_type=F32)
    vv = jnp.dot(kvn, wv_ref[...], preferred_element_type=F32)
    pe0 = kv0 + MLA_KV_RANK
    kpe = (proj_scr[:, pe0:pe0 + V7X_LANES] * cos
           + proj_scr[:, pe0 + V7X_LANES:pe0 + 2 * V7X_LANES] * sin).astype(BF16)
    ones = jnp.ones((tm, V7X_LANES), BF16)
    for hd in range(MLA_HEADS):
        o = hd * QK_PAD
        hs = slice(hd * V7X_LANES, (hd + 1) * V7X_LANES)
        k_ref[:, o:o + V7X_LANES] = kn[:, hs].astype(BF16)
        k_ref[:, o + V7X_LANES:o + QK_PAD] = kpe
        v_ref[:, o:o + V7X_LANES] = vv[:, hs].astype(BF16)
        v_ref[:, o + V7X_LANES:o + QK_PAD] = ones


def _hyb_in(x2, g, sc, sh, win, lng, sw, sb, qg, kvg, wqa, wqb, wk, wv, cos, sin, q_scale):
    s, d = x2.shape
    tm = min(s, 512)
    nproj = win.shape[1]
    row = lambda w: pl.BlockSpec((tm, w), lambda i: (i, 0))
    vec = lambda w: pl.BlockSpec((1, w), lambda i: (0, 0))
    hq = MLA_HEADS * QK_PAD
    return pl.pallas_call(
        functools.partial(_hyb_in_kernel, q_scale=q_scale),
        grid=(s // tm,),
        in_specs=[row(d), vec(d), vec(d), vec(d), _const_spec(win.shape), vec(SGU_WIDTH),
                  _const_spec(sw.shape), _const_spec(sb.shape), vec(MLA_Q_RANK), vec(MLA_KV_RANK),
                  _const_spec(wqa.shape), _const_spec(wqb.shape), _const_spec(wk.shape),
                  _const_spec(wv.shape), row(V7X_LANES), row(V7X_LANES)],
        out_specs=[row(SGU_WIDTH), row(hq), row(hq), row(hq)],
        out_shape=[jax.ShapeDtypeStruct((s, SGU_WIDTH), BF16),
                   jax.ShapeDtypeStruct((s, hq), BF16),
                   jax.ShapeDtypeStruct((s, hq), BF16),
                   jax.ShapeDtypeStruct((s, hq), BF16)],
        scratch_shapes=[pltpu.VMEM((tm, nproj), F32), pltpu.VMEM((tm, SGU_WIDTH), BF16)],
        compiler_params=_cparams(("parallel",), 56),
        name="hyb_in",
    )(x2, g, sc, sh, win, lng, sw, sb, qg, kvg, wqa, wqb, wk, wv, cos, sin)


def _flash_kernel(q_ref, k_ref, v_ref, o_ref, m_scr, acc_scr, *, tk, sub):
    nk = k_ref.shape[0] // tk
    tq = q_ref.shape[0]
    m_scr[...] = jnp.full(m_scr.shape, -jnp.inf, F32)
    acc_scr[...] = jnp.zeros(acc_scr.shape, F32)

    def body(j, carry):
        off = pl.multiple_of(j * tk, tk)
        k = k_ref[pl.ds(off, tk), :]
        v = v_ref[pl.ds(off, tk), :]
        for r in range(tq // sub):
            rows = slice(r * sub, (r + 1) * sub)
            s = lax.dot_general(q_ref[rows, :], k, (((1,), (1,)), ((), ())),
                                preferred_element_type=F32)
            m_prev = m_scr[rows, :]
            m_new = jnp.maximum(m_prev, jnp.max(s, axis=-1, keepdims=True))
            alpha = jnp.exp2(m_prev - m_new)
            p = jnp.exp2(s - jnp.tile(m_new, (1, tk // V7X_LANES)))
            acc_scr[rows, :] = (acc_scr[rows, :] * jnp.tile(alpha, (1, 2))
                                + jnp.dot(p.astype(BF16), v, preferred_element_type=F32))
            m_scr[rows, :] = m_new
        return carry

    lax.fori_loop(0, nk, body, 0, unroll=8 if nk % 8 == 0 else 1)
    acc = acc_scr[...]
    o_ref[...] = (acc[:, :MLA_V] / acc[:, MLA_V:]).astype(BF16)


def _flash(q, k, v):
    s = q.shape[0]
    tq = min(s, 2048)
    tk = min(s, 1024)
    return pl.pallas_call(
        functools.partial(_flash_kernel, tk=tk, sub=min(tq, 256)),
        grid=(MLA_HEADS, s // tq),
        in_specs=[pl.BlockSpec((tq, QK_PAD), lambda h, i: (i, h)),
                  pl.BlockSpec((s, QK_PAD), lambda h, i: (0, h)),
                  pl.BlockSpec((s, QK_PAD), lambda h, i: (0, h))],
        out_specs=pl.BlockSpec((tq, MLA_V), lambda h, i: (i, h)),
        out_shape=jax.ShapeDtypeStruct((s, MLA_HEADS * MLA_V), BF16),
        scratch_shapes=[pltpu.VMEM((tq, V7X_LANES), F32), pltpu.VMEM((tq, QK_PAD), F32)],
        compiler_params=_cparams(("parallel", "arbitrary"), 56),
        name="flash_attn",
    )(q, k, v)


def _hyb_out_kernel(x_ref, a_ref, o_ref, w_ref, gate_ref, y_ref):
    m = (jnp.dot(a_ref[...], w_ref[:SGU_WIDTH, :], preferred_element_type=F32)
         + jnp.dot(o_ref[...], w_ref[SGU_WIDTH:, :], preferred_element_type=F32))
    y_ref[...] = x_ref[...] + gate_ref[...] * m


def _hyb_out(x2, a, o, w, gate):
    s, d = x2.shape
    tm = min(s, 512)
    return pl.pallas_call(
        _hyb_out_kernel,
        grid=(s // tm,),
        in_specs=[pl.BlockSpec((tm, d), lambda i: (i, 0)),
                  pl.BlockSpec((tm, a.shape[1]), lambda i: (i, 0)),
                  pl.BlockSpec((tm, o.shape[1]), lambda i: (i, 0)),
                  _const_spec(w.shape),
                  pl.BlockSpec((1, d), lambda i: (0, 0))],
        out_specs=pl.BlockSpec((tm, d), lambda i: (i, 0)),
        out_shape=jax.ShapeDtypeStruct((s, d), F32),
        compiler_params=_cparams(("parallel",), 48),
        name="hyb_out",
    )(x2, a, o, w, gate)


def _ffn_kernel(x_ref, g_ref, sc_ref, sh_ref, gate_ref, fg_ref, w1_ref, w2_ref, o_ref,
                h_scr, *, final_norm):
    k = pl.program_id(1)

    @pl.when(k == 0)
    def _():
        h_scr[...] = _norm_mod(x_ref[...], g_ref[...], sc_ref[...], sh_ref[...]).astype(BF16)
        o_ref[...] = jnp.zeros(o_ref.shape, F32)

    hid = jnp.dot(h_scr[...], w1_ref[...], preferred_element_type=F32)
    hid = jnp.square(jnp.maximum(hid, 0.0)).astype(BF16)
    o_ref[...] += jnp.dot(hid, w2_ref[...], preferred_element_type=F32)

    @pl.when(k == pl.num_programs(1) - 1)
    def _():
        y = x_ref[...] + gate_ref[...] * o_ref[...]
        if final_norm:
            y = y * lax.rsqrt(jnp.mean(y * y, axis=-1, keepdims=True) + EPS) * fg_ref[...]
        o_ref[...] = y


def _ffn(x2, g, sc, sh, gate, fg, w1, w2, layer, final_norm):
    s, d = x2.shape
    hidden = w1.shape[2]
    tm = min(s, 512)
    tk = 2048
    vec = pl.BlockSpec((1, d), lambda i, k: (0, 0))
    return pl.pallas_call(
        functools.partial(_ffn_kernel, final_norm=final_norm),
        grid=(s // tm, hidden // tk),
        in_specs=[pl.BlockSpec((tm, d), lambda i, k: (i, 0)), vec, vec, vec, vec, vec,
                  pl.BlockSpec((None, d, tk), lambda i, k: (layer, 0, k)),
                  pl.BlockSpec((None, tk, d), lambda i, k: (layer, k, 0))],
        out_specs=pl.BlockSpec((tm, d), lambda i, k: (i, 0)),
        out_shape=jax.ShapeDtypeStruct((s, d), F32),
        scratch_shapes=[pltpu.VMEM((tm, d), BF16)],
        compiler_params=_cparams(("parallel", "arbitrary"), 60),
        name="ffn",
    )(x2, g, sc, sh, gate, fg, w1, w2)


SSM_IN_HALO = V7X_BF16_SUBLANE_TILE
SSM_IN_SUB = 256
SSM_IN_ROWS = 1024


def _ssm_in_kernel(xp_ref, x_ref, xn_ref, g_ref, sc_ref, sh_ref, w_ref, wdt_ref, dtb_ref,
                   cw_ref, cb_ref, z_ref, xbc_ref, dtt_ref, h_scr, *, nz):
    i = pl.program_id(0)
    j = pl.program_id(1)
    tm = x_ref.shape[0]
    halo = xp_ref.shape[0]
    pad = SSM_CONV // 2

    @pl.when(j == 0)
    def _():
        norm = lambda v: _norm_mod(v, g_ref[...], sc_ref[...], sh_ref[...])
        hb = norm(x_ref[...]).astype(BF16)
        h_scr[0:halo, :] = jnp.where(i == 0, 0.0, norm(xp_ref[...])).astype(BF16)
        h_scr[halo:halo + tm, :] = hb
        h_scr[halo + tm:, :] = jnp.where(i == pl.num_programs(0) - 1, 0.0, norm(xn_ref[...])).astype(BF16)
        r = jnp.dot(hb, wdt_ref[...], preferred_element_type=F32) + dtb_ref[...]
        dtt_ref[...] = (jnp.maximum(r, 0.0) + jnp.log1p(jnp.exp(-jnp.abs(r)))).T

    @pl.when(j < nz)
    def _():
        z_ref[...] = jnp.dot(h_scr[halo:halo + tm, :], w_ref[...],
                             preferred_element_type=F32).astype(BF16)

    @pl.when(j >= nz)
    def _():
        rb = min(tm, SSM_IN_ROWS)
        for r in range(tm // rb):
            for c in range(w_ref.shape[1] // SSM_IN_SUB):
                cols = slice(c * SSM_IN_SUB, (c + 1) * SSM_IN_SUB)
                pr = jnp.dot(h_scr[r * rb:r * rb + rb + 2 * halo, :], w_ref[:, cols],
                             preferred_element_type=F32)
                acc = jnp.broadcast_to(cb_ref[:, cols], (rb, SSM_IN_SUB))
                for t in range(SSM_CONV):
                    acc = acc + cw_ref[t:t + 1, cols] * pr[halo - pad + t:halo - pad + t + rb, :]
                y = _silu(acc)
                for u in range(SSM_IN_SUB // V7X_LANES):
                    xbc_ref[c * (SSM_IN_SUB // V7X_LANES) + u, r * rb:(r + 1) * rb, :] = (
                        y[:, u * V7X_LANES:(u + 1) * V7X_LANES].astype(BF16))


def _ssm_in(x2, g, sc, sh, w_in, dt_bias, conv_w, conv_b):
    s, d = x2.shape
    ndt = 2 * SSM_HEADS
    tm = min(s, 1024)
    tn = 1024
    halo = SSM_IN_HALO
    nz = SSM_INNER // tn
    nrow_h = s // halo
    vec = pl.BlockSpec((1, d), lambda i, j: (0, 0))
    ccol = lambda j: jnp.maximum(j - nz, 0)
    return pl.pallas_call(
        functools.partial(_ssm_in_kernel, nz=nz),
        grid=(s // tm, ZX_WIDTH // tn),
        in_specs=[pl.BlockSpec((halo, d), lambda i, j: (jnp.maximum(i * (tm // halo) - 1, 0), 0)),
                  pl.BlockSpec((tm, d), lambda i, j: (i, 0)),
                  pl.BlockSpec((halo, d), lambda i, j: (jnp.minimum((i + 1) * (tm // halo), nrow_h - 1), 0)),
                  vec, vec, vec,
                  pl.BlockSpec((d, tn), lambda i, j: (0, j)),
                  pl.BlockSpec((d, ndt), lambda i, j: (0, ZX_WIDTH // ndt)),
                  pl.BlockSpec((1, ndt), lambda i, j: (0, 0)),
                  pl.BlockSpec((SSM_CONV, tn), lambda i, j: (0, ccol(j))),
                  pl.BlockSpec((1, tn), lambda i, j: (0, ccol(j)))],
        out_specs=[pl.BlockSpec((tm, tn), lambda i, j: (i, jnp.minimum(j, nz - 1))),
                   pl.BlockSpec((tn // V7X_LANES, tm, V7X_LANES), lambda i, j: (ccol(j), i, 0)),
                   pl.BlockSpec((ndt, tm), lambda i, j: (0, i))],
        out_shape=[jax.ShapeDtypeStruct((s, SSM_INNER), BF16),
                   jax.ShapeDtypeStruct((XBC_BLOCKS, s, V7X_LANES), BF16),
                   jax.ShapeDtypeStruct((ndt, s), F32)],
        scratch_shapes=[pltpu.VMEM((tm + 2 * halo, d), BF16)],
        compiler_params=_cparams(("parallel", "arbitrary"), 48),
        name="ssm_in",
    )(x2, x2, x2, g, sc, sh, w_in, w_in, dt_bias, conv_w, conv_b)


def _chunk_cumsum(dta, tri):
    n = dta.shape[0]
    hi = dta.astype(BF16)
    r1 = dta - hi.astype(F32)
    mid = r1.astype(BF16)
    lo = (r1 - mid.astype(F32)).astype(BF16)
    cs = jnp.dot(jnp.concatenate([hi, mid, lo], axis=0), tri, preferred_element_type=F32)
    return cs[0:n] + cs[n:2 * n] + cs[2 * n:3 * n]


def _lane_bcast_col(row, L):
    return jnp.broadcast_to(row, (V7X_LANES, L)).T


def _ssd_kernel(xa_ref, ba_ref, ca_ref, dta_ref, xb_ref, bb_ref, cb_ref, dtb_ref, alog_ref,
                ya_ref, yb_ref, stf_scr, stb_scr):
    L = xa_ref.shape[1]
    P = SSM_HEAD_DIM
    H = SSM_HEADS

    @pl.when(pl.program_id(0) == 0)
    def _():
        stf_scr[...] = jnp.zeros(stf_scr.shape, F32)
        stb_scr[...] = jnp.zeros(stb_scr.shape, F32)

    neg_a = -jnp.exp(alog_ref[...])
    dt_a = dta_ref[...]
    dt_b = dtb_ref[H:, :]
    kk = lax.broadcasted_iota(jnp.int32, (L, L), 0)
    ii = lax.broadcasted_iota(jnp.int32, (L, L), 1)
    tri_f = jnp.where(kk <= ii, 1.0, 0.0).astype(BF16)
    tri_b = jnp.where(kk >= ii, 1.0, 0.0).astype(BF16)
    cf = _chunk_cumsum(dt_a[:H] * neg_a[:H], tri_f) * LOG2E
    rc = _chunk_cumsum(jnp.concatenate([dt_a[H:] * neg_a[H:], dt_b * neg_a[H:]], axis=0), tri_b) * LOG2E
    ra = rc[:H]
    rb = rc[H:]
    cf_dt = cf - jnp.log(dt_a[:H]) * LOG2E
    ra_dt = ra - jnp.log(dt_a[H:]) * LOG2E
    rb_dt = rb - jnp.log(dt_b) * LOG2E
    f_end = cf[:, L - 1:L]
    b_end = rb[:, 0:1]
    to_end_f = jnp.exp2(f_end - cf_dt)
    to_end_b = jnp.exp2(b_end - rb_dt)
    dec_f = jnp.exp2(f_end)
    dec_b = jnp.exp2(b_end)
    below = kk > ii
    diag = kk == ii
    dt_sum = jnp.log(dt_a[:H] + dt_a[H:]) * LOG2E
    low =lax.broadcasted_iota(jnp.int32, (1, V7X_LANES), 1) < P
    heads_per_block = V7X_LANES // P
    reps = (1, L // V7X_LANES)

    for g in range(SSM_GROUPS):
        gl = slice(g * HEADS_PER_GROUP * P, (g + 1) * HEADS_PER_GROUP * P)
        cga = ca_ref[g]
        bta = ba_ref[g].astype(F32).T.astype(BF16)
        cba = jnp.dot(cga, bta, preferred_element_type=F32).astype(BF16)
        stf = stf_scr[:, gl]
        ysf = jnp.dot(cga, stf.astype(BF16), preferred_element_type=F32)
        cgb = cb_ref[g]
        btb = bb_ref[g].astype(F32).T.astype(BF16)
        stb = stb_scr[:, gl]
        ysb = jnp.dot(cgb, stb.astype(BF16), preferred_element_type=F32)
        for q in range(PAIRS_PER_GROUP):
            blk = g * PAIRS_PER_GROUP + q
            xpa = xa_ref[blk]
            xpb = xb_ref[blk]
            lanes = slice(q * V7X_LANES, (q + 1) * V7X_LANES)
            yas, ybs, sfs, sbs = [], [], [], []
            for e in range(heads_per_block):
                hd = blk * heads_per_block + e
                hr = slice(hd, hd + 1)
                cf_col = _lane_bcast_col(cf[hr, :], L)
                ra_col = _lane_bcast_col(ra[hr, :], L)
                rb_col = _lane_bcast_col(rb[hr, :], L)
                arg = jnp.where(below, jnp.tile(cf_col, reps) - cf_dt[hr, :],
                                jnp.where(diag, dt_sum[hr, :], jnp.tile(ra_col, reps) - ra_dt[hr, :]))
                w = jnp.exp2(arg).astype(BF16) * cba
                yh = jnp.dot(w, xpa, preferred_element_type=F32)
                yas.append(yh + ysf[:, lanes] * jnp.exp2(cf_col))
                ybs.append(ysb[:, lanes] * jnp.exp2(rb_col))
                sfs.append(jnp.dot(bta * to_end_f[hr, :].astype(BF16), xpa, preferred_element_type=F32))
                sbs.append(jnp.dot(btb * to_end_b[hr, :].astype(BF16), xpb, preferred_element_type=F32))
            ya_ref[blk] = jnp.where(low, yas[0], yas[1]).astype(BF16)
            yb_ref[blk] = jnp.where(low, ybs[0], ybs[1]).astype(BF16)
            h0 = blk * heads_per_block
            sl = slice(blk * V7X_LANES, (blk + 1) * V7X_LANES)
            decf = jnp.where(low, dec_f[h0:h0 + 1, :], dec_f[h0 + 1:h0 + 2, :])
            decb = jnp.where(low, dec_b[h0:h0 + 1, :], dec_b[h0 + 1:h0 + 2, :])
            stf_scr[:, sl] = stf[:, lanes] * decf + jnp.where(low, sfs[0], sfs[1])
            stb_scr[:, sl] = stb[:, lanes] * decb + jnp.where(low, sbs[0], sbs[1])


def _ssd_scan(xbc_t, dtt, alog_col):
    s = xbc_t.shape[1]
    L = SSM_CHUNK
    nc = s // L
    fwd = lambda t: t
    bwd = lambda t: nc - 1 - t
    b0 = X_BLOCKS // SSM_GROUPS
    c0 = (X_BLOCKS + SSM_GROUPS) // SSM_GROUPS

    def chunk_specs(cidx):
        return [pl.BlockSpec((X_BLOCKS, L, V7X_LANES), lambda t: (0, cidx(t), 0)),
                pl.BlockSpec((SSM_GROUPS, L, V7X_LANES), lambda t: (b0, cidx(t), 0)),
                pl.BlockSpec((SSM_GROUPS, L, V7X_LANES), lambda t: (c0, cidx(t), 0)),
                pl.BlockSpec((2 * SSM_HEADS, L), lambda t: (0, cidx(t)))]

    yspec = lambda cidx: pl.BlockSpec((X_BLOCKS, L, V7X_LANES), lambda t: (0, cidx(t), 0))
    yshape = jax.ShapeDtypeStruct((X_BLOCKS, s, V7X_LANES), BF16)
    return pl.pallas_call(
        _ssd_kernel,
        grid=(nc,),
        in_specs=chunk_specs(fwd) + chunk_specs(bwd)
        + [pl.BlockSpec((2 * SSM_HEADS, 1), lambda t: (0, 0))],
        out_specs=[yspec(fwd), yspec(bwd)],
        out_shape=[yshape, yshape],
        scratch_shapes=[pltpu.VMEM((SSM_STATE, SSM_INNER), F32), pltpu.VMEM((SSM_STATE, SSM_INNER), F32)],
        compiler_params=_cparams(("arbitrary",), 48),
        name="ssd_bidir",
    )(xbc_t, xbc_t, xbc_t, dtt, xbc_t, xbc_t, xbc_t, dtt, alog_col)


def _ssm_gate_norm(g, yf_ref, yb_ref, xs_ref, z_ref, d_ref, ng_ref):
    per_group = PAIRS_PER_GROUP
    gw = SSM_INNER // SSM_GROUPS
    ys = []
    ss = None
    for p in range(per_group):
        j = g * per_group + p
        lanes = slice(j * V7X_LANES, (j + 1) * V7X_LANES)
        y = (yf_ref[j].astype(F32) + yb_ref[j].astype(F32)
             + d_ref[:, lanes] * xs_ref[j].astype(F32))
        y = y * _silu(z_ref[:, lanes].astype(F32))
        ys.append(y)
        t = jnp.sum(y * y, axis=-1, keepdims=True)
        ss = t if ss is None else ss + t
    scale = lax.rsqrt(ss * (1.0 / gw) + EPS)
    pieces = []
    for p in range(per_group):
        j = g * per_group + p
        lanes = slice(j * V7X_LANES, (j + 1) * V7X_LANES)
        pieces.append((ys[p] * scale * ng_ref[:, lanes]).astype(BF16))
    return jnp.concatenate(pieces, axis=1)


def _ssm_out_kernel(x_ref, yf_ref, yb_ref, xs_ref, z_ref, d_ref, ng_ref, w_ref, gate_ref, o_ref):
    gw = SSM_INNER // SSM_GROUPS
    acc = None
    for g in range(SSM_GROUPS):
        yn = _ssm_gate_norm(g, yf_ref, yb_ref, xs_ref, z_ref, d_ref, ng_ref)
        part = jnp.dot(yn, w_ref[g * gw:(g + 1) * gw, :], preferred_element_type=F32)
        acc = part if acc is None else acc + part
    o_ref[...] = x_ref[...] + gate_ref[...] * acc


def _ssm_out(x2, yf, yb, xbc_t, z, d_full, ng, w, gate):
    s, d = x2.shape
    tm = min(s, 256)
    yspec = pl.BlockSpec((X_BLOCKS, tm, V7X_LANES), lambda i: (0, i, 0))
    return pl.pallas_call(
        _ssm_out_kernel,
        grid=(s // tm,),
        in_specs=[pl.BlockSpec((tm, d), lambda i: (i, 0)), yspec, yspec, yspec,
                  pl.BlockSpec((tm, SSM_INNER), lambda i: (i, 0)),
                  pl.BlockSpec((1, SSM_INNER), lambda i: (0, 0)),
                  pl.BlockSpec((1, SSM_INNER), lambda i: (0, 0)),
                  _const_spec(w.shape),
                  pl.BlockSpec((1, d), lambda i: (0, 0))],
        out_specs=pl.BlockSpec((tm, d), lambda i: (i, 0)),
        out_shape=jax.ShapeDtypeStruct((s, d), F32),
        compiler_params=_cparams(("parallel",), 56),
        name="ssm_out",
    )(x2, yf, yb, xbc_t, z, d_full, ng, w, gate)


def _rot_cols(w):
    half = MLA_ROPE // 2
    return jnp.concatenate([-w[..., half:], w[..., :half]], axis=-1)


def _pad_lanes(w, width):
    return jnp.pad(w, [(0, 0)] * (w.ndim - 1) + [(0, width - w.shape[-1])])


def kernel(x, c, positions, ada_w, ada_b, norm_mix_g, norm_ffn_g, ffn_w1, ffn_w2, hyb_w_in, sgu_norm_g, sgu_w, sgu_b, mla_q_norm_g, mla_kv_norm_g, mla_w_uq, mla_w_ukv, hyb_w_out, ssm_w_in, ssm_conv_w, ssm_conv_b, ssm_dt_bias, ssm_a_log, ssm_d, ssm_norm_g, ssm_w_out, final_norm_g):
    batch, s, d = x.shape
    assert batch == 1 and d == D_MODEL and s % 1024 == 0
    depth = ada_w.shape[0]
    assert depth == 2
    x2 = x.reshape(s, d)

    mod = _ada_mod(c, ada_w, ada_b)
    mods = [[mod[l, :, i * d:(i + 1) * d] for i in range(6)] for l in range(depth)]
    row = lambda v: v.reshape(1, -1)

    sh1, sc1, g1, sh2, sc2, g2 = mods[0]
    w_in = hyb_w_in[0]
    c_pe = 2 * SGU_WIDTH + MLA_Q_RANK + MLA_KV_RANK
    w_kpe = w_in[:, c_pe:]
    win_ext = jnp.concatenate(
        [w_in[:, :c_pe], _pad_lanes(w_kpe, V7X_LANES), _pad_lanes(_rot_cols(w_kpe), V7X_LANES)],
        axis=1).astype(BF16)
    wq = mla_w_uq[0].reshape(MLA_Q_RANK, MLA_HEADS, MLA_NOPE + MLA_ROPE)
    wqa = _pad_lanes(wq, QK_PAD).reshape(MLA_Q_RANK, MLA_HEADS * QK_PAD).astype(BF16)
    wqb = _pad_lanes(_rot_cols(wq[..., MLA_NOPE:]), V7X_LANES).reshape(
        MLA_Q_RANK, MLA_HEADS * V7X_LANES).astype(BF16)
    wkv = mla_w_ukv[0].reshape(MLA_KV_RANK, MLA_HEADS, MLA_NOPE + MLA_V)
    wk = wkv[..., :MLA_NOPE].reshape(MLA_KV_RANK, MLA_HEADS * MLA_NOPE).astype(BF16)
    wv = wkv[..., MLA_NOPE:].reshape(MLA_KV_RANK, MLA_HEADS * MLA_V).astype(BF16)
    sb_full = jnp.repeat(sgu_b[0].T, SGU_HEAD, axis=1)

    half = MLA_ROPE // 2
    inv_freq = ROPE_THETA ** (-jnp.arange(half, dtype=F32) / half)
    inv128 = jnp.tile(inv_freq, V7X_LANES // half).reshape(1, V7X_LANES)
    cos, sin = _rope_tables(positions.astype(F32).reshape(s, 1), inv128)
    q_scale = (MLA_NOPE + MLA_ROPE) ** -0.5 * LOG2E

    a_out, q, k, v = _hyb_in(x2, row(norm_mix_g[0]), sc1, sh1, win_ext, row(sgu_norm_g[0]),
                             sgu_w[0].astype(BF16), sb_full, row(mla_q_norm_g[0]),
                             row(mla_kv_norm_g[0]), wqa, wqb, wk, wv, cos, sin, q_scale)
    o = _flash(q, k, v)
    x2 = _hyb_out(x2, a_out, o, hyb_w_out[0].astype(BF16), g1)
    w1_bf = ffn_w1.astype(BF16)
    w2_bf = ffn_w2.astype(BF16)
    x2 = _ffn(x2, row(norm_ffn_g[0]), sc2, sh2, g2, row(final_norm_g), w1_bf, w2_bf, 0, False)

    sh1, sc1, g1, sh2, sc2, g2 = mods[1]
    z, xbc_t, dtt = _ssm_in(x2, row(norm_mix_g[1]), sc1, sh1, ssm_w_in[0].astype(BF16),
                            ssm_dt_bias[0].reshape(1, 2 * SSM_HEADS),
                            ssm_conv_w[0], row(ssm_conv_b[0]))
    alog_col = ssm_a_log[0].reshape(2 * SSM_HEADS, 1)
    yf, yb = _ssd_scan(xbc_t, dtt, alog_col)
    d_full = jnp.repeat(ssm_d[0], SSM_HEAD_DIM).reshape(1, SSM_INNER)
    x2 = _ssm_out(x2, yf, yb, xbc_t, z, d_full, row(ssm_norm_g[0]), ssm_w_out[0].astype(BF16), g1)
    x2 = _ffn(x2, row(norm_ffn_g[1]), sc2, sh2, g2, row(final_norm_g), w1_bf, w2_bf, 1, True)
    return x2.reshape(batch, s, d)
```

```python
import functools
import math

import jax
import jax.numpy as jnp
from jax import lax
from jax.experimental import pallas as pl
from jax.experimental.pallas import tpu as pltpu

F32 = jnp.float32
BF16 = jnp.bfloat16

V7X_LANES = 128
V7X_BF16_SUBLANE_TILE = 16
V7X_VMEM_BYTES = 64 * 1024 * 1024

D_MODEL = 2048
SGU_CHUNK = 128
SGU_GROUPS = 8
SGU_HEAD = 128
SGU_WIDTH = SGU_GROUPS * SGU_HEAD
MLA_HEADS = 8
MLA_Q_RANK = 512
MLA_KV_RANK = 512
MLA_NOPE = 128
MLA_ROPE = 64
MLA_V = 128
ROPE_THETA = 10000.0
HYB_MIX = SGU_WIDTH + MLA_HEADS * MLA_V
SSM_INNER = 2 * D_MODEL
SSM_HEAD_DIM = 64
SSM_HEADS = SSM_INNER // SSM_HEAD_DIM
SSM_GROUPS = 8
SSM_STATE = 128
SSM_CONV = 5
SSM_CHUNK = 256
SSM_CONV_CH = SSM_INNER + 2 * SSM_GROUPS * SSM_STATE
FFN_HIDDEN = 4 * D_MODEL
EPS = 1e-6

QK_PAD = 2 * V7X_LANES
HEADS_PER_GROUP = SSM_HEADS // SSM_GROUPS
PAIRS_PER_GROUP = HEADS_PER_GROUP * SSM_HEAD_DIM // V7X_LANES
X_BLOCKS = SSM_INNER // V7X_LANES
XBC_BLOCKS = SSM_CONV_CH // V7X_LANES
ZX_WIDTH = SSM_INNER + SSM_CONV_CH
LOG2E = 1.4426950408889634


def _cparams(sem, vmem_mib):
    return pltpu.CompilerParams(dimension_semantics=sem,
                                vmem_limit_bytes=vmem_mib * 1024 * 1024)


def _const_spec(shape):
    nd = len(shape)
    return pl.BlockSpec(shape, lambda *_: (0,) * nd, pipeline_mode=pl.Buffered(1))


def _norm_mod(x, g, sc, sh):
    y = x * lax.rsqrt(jnp.mean(x * x, axis=-1, keepdims=True) + EPS)
    return (y * g) * (1.0 + sc) + sh


def _silu(x):
    return x * jax.nn.sigmoid(x)


def _ada_kernel(c_ref, w_ref, b_ref, o_ref):
    c = c_ref[...]
    d = c.shape[-1]
    tn = w_ref.shape[2]
    cond_col = jnp.broadcast_to(_silu(c), (V7X_LANES, d)).T
    prod = w_ref[0] * jnp.tile(cond_col, (1, tn // V7X_LANES))
    o_ref[0] = jnp.sum(prod, axis=0, keepdims=True) + b_ref[0]


def _ada_mod(c, ada_w, ada_b):
    depth, d, n = ada_w.shape
    tn = 1024
    return pl.pallas_call(
        _ada_kernel,
        grid=(depth, n // tn),
        in_specs=[pl.BlockSpec((1, d), lambda l, j: (0, 0)),
                  pl.BlockSpec((1, d, tn), lambda l, j: (l, 0, j)),
                  pl.BlockSpec((1, 1, tn), lambda l, j: (l, 0, j))],
        out_specs=pl.BlockSpec((1, 1, tn), lambda l, j: (l, 0, j)),
        out_shape=jax.ShapeDtypeStruct((depth, 1, n), F32),
        compiler_params=_cparams(("parallel", "parallel"), 40),
        name="ada_mod",
    )(c, ada_w, ada_b.reshape(depth, 1, n))


def _rope_kernel(pos_ref, inv_ref, cos_ref, sin_ref):
    ang = pos_ref[...] * inv_ref[...]
    cos_ref[...] = jnp.cos(ang)
    sin_ref[...] = jnp.sin(ang)


def _rope_tables(pos_col, inv128):
    s = pos_col.shape[0]
    tm = min(s, 2048)
    return pl.pallas_call(
        _rope_kernel,
        grid=(s // tm,),
        in_specs=[pl.BlockSpec((tm, 1), lambda i: (i, 0)),
                  pl.BlockSpec((1, V7X_LANES), lambda i: (0, 0))],
        out_specs=[pl.BlockSpec((tm, V7X_LANES), lambda i: (i, 0))] * 2,
        out_shape=[jax.ShapeDtypeStruct((s, V7X_LANES), F32)] * 2,
        compiler_params=_cparams(("parallel",), 32),
        name="rope_tables",
    )(pos_col, inv128)


def _hyb_in_kernel(x_ref, g_ref, sc_ref, sh_ref, win_ref, lng_ref, sw_ref, sb_ref,
                   qg_ref, kvg_ref, wqa_ref, wqb_ref, wk_ref, wv_ref, cos_ref, sin_ref,
                   a_ref, q_ref, k_ref, v_ref, proj_scr, vn_scr, *, q_scale):
    tm = x_ref.shape[0]
    h = _norm_mod(x_ref[...], g_ref[...], sc_ref[...], sh_ref[...])
    proj_scr[...] = jnp.dot(h.astype(BF16), win_ref[...], preferred_element_type=F32)

    v = jax.nn.gelu(proj_scr[:, SGU_WIDTH:2 * SGU_WIDTH])
    vc = v - jnp.mean(v, axis=-1, keepdims=True)
    vn = vc * lax.rsqrt(jnp.mean(vc * vc, axis=-1, keepdims=True) + EPS) * lng_ref[...]
    vn_scr[...] = vn.astype(BF16)
    for c in range(tm // SGU_CHUNK):
        rows = slice(c * SGU_CHUNK, (c + 1) * SGU_CHUNK)
        for g in range(SGU_GROUPS):
            cols = slice(g * SGU_HEAD, (g + 1) * SGU_HEAD)
            mixed = jnp.dot(sw_ref[g], vn_scr[rows, cols], preferred_element_type=F32)
            u = jax.nn.gelu(proj_scr[rows, cols])
            a_ref[rows, cols] = (u * (mixed + sb_ref[:, cols])).astype(BF16)

    cos = cos_ref[...]
    sin = sin_ref[...]
    lat0 = 2 * SGU_WIDTH
    ql = proj_scr[:, lat0:lat0 + MLA_Q_RANK]
    qn = (ql * lax.rsqrt(jnp.mean(ql * ql, axis=-1, keepdims=True) + EPS) * qg_ref[...]).astype(BF16)
    qa = jnp.dot(qn, wqa_ref[...], preferred_element_type=F32)
    qb = jnp.dot(qn, wqb_ref[...], preferred_element_type=F32)
    for hd in range(MLA_HEADS):
        o = hd * QK_PAD
        q_ref[:, o:o + V7X_LANES] = (qa[:, o:o + V7X_LANES] * q_scale).astype(BF16)
        pe = (qa[:, o + V7X_LANES:o + QK_PAD] * cos
              + qb[:, hd * V7X_LANES:(hd + 1) * V7X_LANES] * sin)
        q_ref[:, o + V7X_LANES:o + QK_PAD] = (pe * q_scale).astype(BF16)

    kv0 = lat0 + MLA_Q_RANK
    kvl = proj_scr[:, kv0:kv0 + MLA_KV_RANK]
    kvn = (kvl * lax.rsqrt(jnp.mean(kvl * kvl, axis=-1, keepdims=True) + EPS) * kvg_ref[...]).astype(BF16)
    kn = jnp.dot(kvn, wk_ref[...], preferred_element_type=F32)
    vv = jnp.dot(kvn, wv_ref[...], preferred_element_type=F32)
    pe0 = kv0 + MLA_KV_RANK
    kpe = (proj_scr[:, pe0:pe0 + V7X_LANES] * cos
           + proj_scr[:, pe0 + V7X_LANES:pe0 + 2 * V7X_LANES] * sin).astype(BF16)
    ones = jnp.ones((tm, V7X_LANES), BF16)
    for hd in range(MLA_HEADS):
        o = hd * QK_PAD
        hs = slice(hd * V7X_LANES, (hd + 1) * V7X_LANES)
        k_ref[:, o:o + V7X_LANES] = kn[:, hs].astype(BF16)
        k_ref[:, o + V7X_LANES:o + QK_PAD] = kpe
        v_ref[:, o:o + V7X_LANES] = vv[:, hs].astype(BF16)
        v_ref[:, o + V7X_LANES:o + QK_PAD] = ones


def _hyb_in(x2, g, sc, sh, win, lng, sw, sb, qg, kvg, wqa, wqb, wk, wv, cos, sin, q_scale):
    s, d = x2.shape
    tm = min(s, 512)
    nproj = win.shape[1]
    row = lambda w: pl.BlockSpec((tm, w), lambda i: (i, 0))
    vec = lambda w: pl.BlockSpec((1, w), lambda i: (0, 0))
    hq = MLA_HEADS * QK_PAD
    return pl.pallas_call(
        functools.partial(_hyb_in_kernel, q_scale=q_scale),
        grid=(s // tm,),
        in_specs=[row(d), vec(d), vec(d), vec(d), _const_spec(win.shape), vec(SGU_WIDTH),
                  _const_spec(sw.shape), _const_spec(sb.shape), vec(MLA_Q_RANK), vec(MLA_KV_RANK),
                  _const_spec(wqa.shape), _const_spec(wqb.shape), _const_spec(wk.shape),
                  _const_spec(wv.shape), row(V7X_LANES), row(V7X_LANES)],
        out_specs=[row(SGU_WIDTH), row(hq), row(hq), row(hq)],
        out_shape=[jax.ShapeDtypeStruct((s, SGU_WIDTH), BF16),
                   jax.ShapeDtypeStruct((s, hq), BF16),
                   jax.ShapeDtypeStruct((s, hq), BF16),
                   jax.ShapeDtypeStruct((s, hq), BF16)],
        scratch_shapes=[pltpu.VMEM((tm, nproj), F32), pltpu.VMEM((tm, SGU_WIDTH), BF16)],
        compiler_params=_cparams(("parallel",), 56),
        name="hyb_in",
    )(x2, g, sc, sh, win, lng, sw, sb, qg, kvg, wqa, wqb, wk, wv, cos, sin)


def _flash_kernel(q_ref, k_ref, v_ref, o_ref, m_scr, acc_scr, *, tk, sub):
    nk = k_ref.shape[0] // tk
    tq = q_ref.shape[0]
    m_scr[...] = jnp.full(m_scr.shape, -jnp.inf, F32)
    acc_scr[...] = jnp.zeros(acc_scr.shape, F32)

    def body(j, carry):
        off = pl.multiple_of(j * tk, tk)
        k = k_ref[pl.ds(off, tk), :]
        v = v_ref[pl.ds(off, tk), :]
        for r in range(tq // sub):
            rows = slice(r * sub, (r + 1) * sub)
            s = lax.dot_general(q_ref[rows, :], k, (((1,), (1,)), ((), ())),
                                preferred_element_type=F32)
            m_prev = m_scr[rows, :]
            m_new = jnp.maximum(m_prev, jnp.max(s, axis=-1, keepdims=True))
            alpha = jnp.exp2(m_prev - m_new)
            p = jnp.exp2(s - jnp.tile(m_new, (1, tk // V7X_LANES)))
            acc_scr[rows, :] = (acc_scr[rows, :] * jnp.tile(alpha, (1, 2))
                                + jnp.dot(p.astype(BF16), v, preferred_element_type=F32))
            m_scr[rows, :] = m_new
        return carry

    lax.fori_loop(0, nk, body, 0, unroll=8 if nk % 8 == 0 else 1)
    acc = acc_scr[...]
    o_ref[...] = (acc[:, :MLA_V] / acc[:, MLA_V:]).astype(BF16)


def _flash(q, k, v):
    s = q.shape[0]
    tq = min(s, 2048)
    tk = min(s, 1024)
    return pl.pallas_call(
        functools.partial(_flash_kernel, tk=tk, sub=min(tq, 256)),
        grid=(MLA_HEADS, s // tq),
        in_specs=[pl.BlockSpec((tq, QK_PAD), lambda h, i: (i, h)),
                  pl.BlockSpec((s, QK_PAD), lambda h, i: (0, h)),
                  pl.BlockSpec((s, QK_PAD), lambda h, i: (0, h))],
        out_specs=pl.BlockSpec((tq, MLA_V), lambda h, i: (i, h)),
        out_shape=jax.ShapeDtypeStruct((s, MLA_HEADS * MLA_V), BF16),
        scratch_shapes=[pltpu.VMEM((tq, V7X_LANES), F32), pltpu.VMEM((tq, QK_PAD), F32)],
        compiler_params=_cparams(("parallel", "arbitrary"), 56),
        name="flash_attn",
    )(q, k, v)


def _hyb_out_kernel(x_ref, a_ref, o_ref, w_ref, gate_ref, y_ref):
    m = (jnp.dot(a_ref[...], w_ref[:SGU_WIDTH, :], preferred_element_type=F32)
         + jnp.dot(o_ref[...], w_ref[SGU_WIDTH:, :], preferred_element_type=F32))
    y_ref[...] = x_ref[...] + gate_ref[...] * m


def _hyb_out(x2, a, o, w, gate):
    s, d = x2.shape
    tm = min(s, 512)
    return pl.pallas_call(
        _hyb_out_kernel,
        grid=(s // tm,),
        in_specs=[pl.BlockSpec((tm, d), lambda i: (i, 0)),
                  pl.BlockSpec((tm, a.shape[1]), lambda i: (i, 0)),
                  pl.BlockSpec((tm, o.shape[1]), lambda i: (i, 0)),
                  _const_spec(w.shape),
                  pl.BlockSpec((1, d), lambda i: (0, 0))],
        out_specs=pl.BlockSpec((tm, d), lambda i: (i, 0)),
        out_shape=jax.ShapeDtypeStruct((s, d), F32),
        compiler_params=_cparams(("parallel",), 48),
        name="hyb_out",
    )(x2, a, o, w, gate)


def _ffn_kernel(x_ref, g_ref, sc_ref, sh_ref, gate_ref, fg_ref, w1_ref, w2_ref, o_ref,
                h_scr, *, final_norm):
    k = pl.program_id(1)

    @pl.when(k == 0)
    def _():
        h_scr[...] = _norm_mod(x_ref[...], g_ref[...], sc_ref[...], sh_ref[...]).astype(BF16)
        o_ref[...] = jnp.zeros(o_ref.shape, F32)

    hid = jnp.dot(h_scr[...], w1_ref[...], preferred_element_type=F32)
    hid = jnp.square(jnp.maximum(hid, 0.0)).astype(BF16)
    o_ref[...] += jnp.dot(hid, w2_ref[...], preferred_element_type=F32)

    @pl.when(k == pl.num_programs(1) - 1)
    def _():
        y = x_ref[...] + gate_ref[...] * o_ref[...]
        if final_norm:
            y = y * lax.rsqrt(jnp.mean(y * y, axis=-1, keepdims=True) + EPS) * fg_ref[...]
        o_ref[...] = y


def _ffn(x2, g, sc, sh, gate, fg, w1, w2, layer, final_norm):
    s, d = x2.shape
    hidden = w1.shape[2]
    tm = min(s, 512)
    tk = 2048
    vec = pl.BlockSpec((1, d), lambda i, k: (0, 0))
    return pl.pallas_call(
        functools.partial(_ffn_kernel, final_norm=final_norm),
        grid=(s // tm, hidden // tk),
        in_specs=[pl.BlockSpec((tm, d), lambda i, k: (i, 0)), vec, vec, vec, vec, vec,
                  pl.BlockSpec((None, d, tk), lambda i, k: (layer, 0, k)),
                  pl.BlockSpec((None, tk, d), lambda i, k: (layer, k, 0))],
        out_specs=pl.BlockSpec((tm, d), lambda i, k: (i, 0)),
        out_shape=jax.ShapeDtypeStruct((s, d), F32),
        scratch_shapes=[pltpu.VMEM((tm, d), BF16)],
        compiler_params=_cparams(("parallel", "arbitrary"), 60),
        name="ffn",
    )(x2, g, sc, sh, gate, fg, w1, w2)


SSM_IN_HALO = V7X_BF16_SUBLANE_TILE
SSM_IN_SUB = 256
SSM_IN_ROWS = 1024


def _ssm_in_kernel(xp_ref, x_ref, xn_ref, g_ref, sc_ref, sh_ref, w_ref, wdt_ref, dtb_ref,
                   cw_ref, cb_ref, z_ref, xbc_ref, dtt_ref, h_scr, *, nz):
    i = pl.program_id(0)
    j = pl.program_id(1)
    tm = x_ref.shape[0]
    halo = xp_ref.shape[0]
    pad = SSM_CONV // 2

    @pl.when(j == 0)
    def _():
        norm = lambda v: _norm_mod(v, g_ref[...], sc_ref[...], sh_ref[...])
        hb = norm(x_ref[...]).astype(BF16)
        h_scr[0:halo, :] = jnp.where(i == 0, 0.0, norm(xp_ref[...])).astype(BF16)
        h_scr[halo:halo + tm, :] = hb
        h_scr[halo + tm:, :] = jnp.where(i == pl.num_programs(0) - 1, 0.0, norm(xn_ref[...])).astype(BF16)
        r = jnp.dot(hb, wdt_ref[...], preferred_element_type=F32) + dtb_ref[...]
        dtt_ref[...] = (jnp.maximum(r, 0.0) + jnp.log1p(jnp.exp(-jnp.abs(r)))).T

    @pl.when(j < nz)
    def _():
        z_ref[...] = jnp.dot(h_scr[halo:halo + tm, :], w_ref[...],
                             preferred_element_type=F32).astype(BF16)

    @pl.when(j >= nz)
    def _():
        rb = min(tm, SSM_IN_ROWS)
        for r in range(tm // rb):
            for c in range(w_ref.shape[1] // SSM_IN_SUB):
                cols = slice(c * SSM_IN_SUB, (c + 1) * SSM_IN_SUB)
                pr = jnp.dot(h_scr[r * rb:r * rb + rb + 2 * halo, :], w_ref[:, cols],
                             preferred_element_type=F32)
                acc = jnp.broadcast_to(cb_ref[:, cols], (rb, SSM_IN_SUB))
                for t in range(SSM_CONV):
                    acc = acc + cw_ref[t:t + 1, cols] * pr[halo - pad + t:halo - pad + t + rb, :]
                y = _silu(acc)
                for u in range(SSM_IN_SUB // V7X_LANES):
                    xbc_ref[c * (SSM_IN_SUB // V7X_LANES) + u, r * rb:(r + 1) * rb, :] = (
                        y[:, u * V7X_LANES:(u + 1) * V7X_LANES].astype(BF16))


def _ssm_in(x2, g, sc, sh, w_in, dt_bias, conv_w, conv_b):
    s, d = x2.shape
    ndt = 2 * SSM_HEADS
    tm = min(s, 1024)
    tn = 1024
    halo = SSM_IN_HALO
    nz = SSM_INNER // tn
    nrow_h = s // halo
    vec = pl.BlockSpec((1, d), lambda i, j: (0, 0))
    ccol = lambda j: jnp.maximum(j - nz, 0)
    return pl.pallas_call(
        functools.partial(_ssm_in_kernel, nz=nz),
        grid=(s // tm, ZX_WIDTH // tn),
        in_specs=[pl.BlockSpec((halo, d), lambda i, j: (jnp.maximum(i * (tm // halo) - 1, 0), 0)),
                  pl.BlockSpec((tm, d), lambda i, j: (i, 0)),
                  pl.BlockSpec((halo, d), lambda i, j: (jnp.minimum((i + 1) * (tm // halo), nrow_h - 1), 0)),
                  vec, vec, vec,
                  pl.BlockSpec((d, tn), lambda i, j: (0, j)),
                  pl.BlockSpec((d, ndt), lambda i, j: (0, ZX_WIDTH // ndt)),
                  pl.BlockSpec((1, ndt), lambda i, j: (0, 0)),
                  pl.BlockSpec((SSM_CONV, tn), lambda i, j: (0, ccol(j))),
                  pl.BlockSpec((1, tn), lambda i, j: (0, ccol(j)))],
        out_specs=[pl.BlockSpec((tm, tn), lambda i, j: (i, jnp.minimum(j, nz - 1))),
                   pl.BlockSpec((tn // V7X_LANES, tm, V7X_LANES), lambda i, j: (ccol(j), i, 0)),
                   pl.BlockSpec((ndt, tm), lambda i, j: (0, i))],
        out_shape=[jax.ShapeDtypeStruct((s, SSM_INNER), BF16),
                   jax.ShapeDtypeStruct((XBC_BLOCKS, s, V7X_LANES), BF16),
                   jax.ShapeDtypeStruct((ndt, s), F32)],
        scratch_shapes=[pltpu.VMEM((tm + 2 * halo, d), BF16)],
        compiler_params=_cparams(("parallel", "arbitrary"), 48),
        name="ssm_in",
    )(x2, x2, x2, g, sc, sh, w_in, w_in, dt_bias, conv_w, conv_b)


def _chunk_cumsum(dta, tri):
    n = dta.shape[0]
    hi = dta.astype(BF16)
    r1 = dta - hi.astype(F32)
    mid = r1.astype(BF16)
    lo = (r1 - mid.astype(F32)).astype(BF16)
    cs = jnp.dot(jnp.concatenate([hi, mid, lo], axis=0), tri, preferred_element_type=F32)
    return cs[0:n] + cs[n:2 * n] + cs[2 * n:3 * n]


def _lane_bcast_col(row, L):
    return jnp.broadcast_to(row, (V7X_LANES, L)).T


def _ssd_kernel(xa_ref, ba_ref, ca_ref, dta_ref, xb_ref, bb_ref, cb_ref, dtb_ref, alog_ref, d_ref,
                ya_ref, yb_ref, stf_scr, stb_scr):
    L = xa_ref.shape[1]
    P = SSM_HEAD_DIM
    H = SSM_HEADS

    @pl.when(pl.program_id(0) == 0)
    def _():
        stf_scr[...] = jnp.zeros(stf_scr.shape, F32)
        stb_scr[...] = jnp.zeros(stb_scr.shape, F32)

    neg_a = -jnp.exp(alog_ref[...])
    dt_a = dta_ref[...]
    dt_b = dtb_ref[H:, :]
    kk = lax.broadcasted_iota(jnp.int32, (L, L), 0)
    ii = lax.broadcasted_iota(jnp.int32, (L, L), 1)
    tri_f = jnp.where(kk <= ii, 1.0, 0.0).astype(BF16)
    tri_b = jnp.where(kk >= ii, 1.0, 0.0).astype(BF16)
    cf = _chunk_cumsum(dt_a[:H] * neg_a[:H], tri_f) * LOG2E
    rc = _chunk_cumsum(jnp.concatenate([dt_a[H:] * neg_a[H:], dt_b * neg_a[H:]], axis=0), tri_b) * LOG2E
    ra = rc[:H]
    rb = rc[H:]
    cf_dt = cf - jnp.log(dt_a[:H]) * LOG2E
    ra_dt = ra - jnp.log(dt_a[H:]) * LOG2E
    rb_dt = rb - jnp.log(dt_b) * LOG2E
    f_end = cf[:, L - 1:L]
    b_end = rb[:, 0:1]
    to_end_f = jnp.exp2(f_end - cf_dt)
    to_end_b = jnp.exp2(b_end - rb_dt)
    dec_f = jnp.exp2(f_end)
    dec_b = jnp.exp2(b_end)
    below = kk > ii
    diag = kk == ii
    dt_sum = jnp.log(dt_a[:H] + dt_a[H:]) * LOG2E
    low =lax.broadcasted_iota(jnp.int32, (1, V7X_LANES), 1) < P
    heads_per_block = V7X_LANES // P
    reps = (1, L // V7X_LANES)

    for g in range(SSM_GROUPS):
        gl = slice(g * HEADS_PER_GROUP * P, (g + 1) * HEADS_PER_GROUP * P)
        cga = ca_ref[g]
        bta = ba_ref[g].astype(F32).T.astype(BF16)
        cba = jnp.dot(cga, bta, preferred_element_type=F32).astype(BF16)
        stf = stf_scr[:, gl]
        ysf = jnp.dot(cga, stf.astype(BF16), preferred_element_type=F32)
        cgb = cb_ref[g]
        btb = bb_ref[g].astype(F32).T.astype(BF16)
        stb = stb_scr[:, gl]
        ysb = jnp.dot(cgb, stb.astype(BF16), preferred_element_type=F32)
        for q in range(PAIRS_PER_GROUP):
            blk = g * PAIRS_PER_GROUP + q
            xpa = xa_ref[blk]
            xpb = xb_ref[blk]
            lanes = slice(q * V7X_LANES, (q + 1) * V7X_LANES)
            yas, ybs, sfs, sbs = [], [], [], []
            for e in range(heads_per_block):
                hd = blk * heads_per_block + e
                hr = slice(hd, hd + 1)
                cf_col = _lane_bcast_col(cf[hr, :], L)
                ra_col = _lane_bcast_col(ra[hr, :], L)
                rb_col = _lane_bcast_col(rb[hr, :], L)
                arg = jnp.where(below, jnp.tile(cf_col, reps) - cf_dt[hr, :],
                                jnp.where(diag, dt_sum[hr, :], jnp.tile(ra_col, reps) - ra_dt[hr, :]))
                w = jnp.exp2(arg).astype(BF16) * cba
                yh = jnp.dot(w, xpa, preferred_element_type=F32)
                yas.append(yh + ysf[:, lanes] * jnp.exp2(cf_col))
                ybs.append(ysb[:, lanes] * jnp.exp2(rb_col))
                sfs.append(jnp.dot(bta * to_end_f[hr, :].astype(BF16), xpa, preferred_element_type=F32))
                sbs.append(jnp.dot(btb * to_end_b[hr, :].astype(BF16), xpb, preferred_element_type=F32))
            sl = slice(blk * V7X_LANES, (blk + 1) * V7X_LANES)
            ya = jnp.where(low, yas[0], yas[1]) + d_ref[:, sl] * xpa.astype(F32)
            ya_ref[blk] = ya.astype(BF16)
            yb_ref[blk] = jnp.where(low, ybs[0], ybs[1]).astype(BF16)
            h0 = blk * heads_per_block
            decf = jnp.where(low, dec_f[h0:h0 + 1, :], dec_f[h0 + 1:h0 + 2, :])
            decb = jnp.where(low, dec_b[h0:h0 + 1, :], dec_b[h0 + 1:h0 + 2, :])
            stf_scr[:, sl] = stf[:, lanes] * decf + jnp.where(low, sfs[0], sfs[1])
            stb_scr[:, sl] = stb[:, lanes] * decb + jnp.where(low, sbs[0], sbs[1])


def _ssd_scan(xbc_t, dtt, alog_col, d_full):
    s = xbc_t.shape[1]
    L = SSM_CHUNK
    nc = s // L
    fwd = lambda t: t
    bwd = lambda t: nc - 1 - t
    b0 = X_BLOCKS // SSM_GROUPS
    c0 = (X_BLOCKS + SSM_GROUPS) // SSM_GROUPS

    def chunk_specs(cidx):
        return [pl.BlockSpec((X_BLOCKS, L, V7X_LANES), lambda t: (0, cidx(t), 0)),
                pl.BlockSpec((SSM_GROUPS, L, V7X_LANES), lambda t: (b0, cidx(t), 0)),
                pl.BlockSpec((SSM_GROUPS, L, V7X_LANES), lambda t: (c0, cidx(t), 0)),
                pl.BlockSpec((2 * SSM_HEADS, L), lambda t: (0, cidx(t)))]

    yspec = lambda cidx: pl.BlockSpec((X_BLOCKS, L, V7X_LANES), lambda t: (0, cidx(t), 0))
    yshape = jax.ShapeDtypeStruct((X_BLOCKS, s, V7X_LANES), BF16)
    return pl.pallas_call(
        _ssd_kernel,
        grid=(nc,),
        in_specs=chunk_specs(fwd) + chunk_specs(bwd)
        + [pl.BlockSpec((2 * SSM_HEADS, 1), lambda t: (0, 0)),
           pl.BlockSpec((1, SSM_INNER), lambda t: (0, 0))],
        out_specs=[yspec(fwd), yspec(bwd)],
        out_shape=[yshape, yshape],
        scratch_shapes=[pltpu.VMEM((SSM_STATE, SSM_INNER), F32), pltpu.VMEM((SSM_STATE, SSM_INNER), F32)],
        compiler_params=_cparams(("arbitrary",), 48),
        name="ssd_bidir",
    )(xbc_t, xbc_t, xbc_t, dtt, xbc_t, xbc_t, xbc_t, dtt, alog_col, d_full)


def _ssm_gate_norm(g, ya_ref, yb_ref, z_ref, ng_ref):
    per_group = PAIRS_PER_GROUP
    gw = SSM_INNER // SSM_GROUPS
    ys = []
    ss = None
    for p in range(per_group):
        j = g * per_group + p
        lanes = slice(j * V7X_LANES, (j + 1) * V7X_LANES)
        y = ya_ref[j].astype(F32) + yb_ref[j].astype(F32)
        y = y * _silu(z_ref[:, lanes].astype(F32))
        ys.append(y)
        t = jnp.sum(y * y, axis=-1, keepdims=True)
        ss = t if ss is None else ss + t
    scale = lax.rsqrt(ss * (1.0 / gw) + EPS)
    pieces = []
    for p in range(per_group):
        j = g * per_group + p
        lanes = slice(j * V7X_LANES, (j + 1) * V7X_LANES)
        pieces.append((ys[p] * scale * ng_ref[:, lanes]).astype(BF16))
    return jnp.concatenate(pieces, axis=1)


def _ssm_out_kernel(x_ref, ya_ref, yb_ref, z_ref, ng_ref, w_ref, gate_ref, o_ref):
    gw = SSM_INNER // SSM_GROUPS
    acc = None
    for g in range(SSM_GROUPS):
        yn = _ssm_gate_norm(g, ya_ref, yb_ref, z_ref, ng_ref)
        part = jnp.dot(yn, w_ref[g * gw:(g + 1) * gw, :], preferred_element_type=F32)
        acc = part if acc is None else acc + part
    o_ref[...] = x_ref[...] + gate_ref[...] * acc


def _ssm_out(x2, ya, yb, z, ng, w, gate):
    s, d = x2.shape
    tm = min(s, 256)
    yspec = pl.BlockSpec((X_BLOCKS, tm, V7X_LANES), lambda i: (0, i, 0))
    return pl.pallas_call(
        _ssm_out_kernel,
        grid=(s // tm,),
        in_specs=[pl.BlockSpec((tm, d), lambda i: (i, 0)), yspec, yspec,
                  pl.BlockSpec((tm, SSM_INNER), lambda i: (i, 0)),
                  pl.BlockSpec((1, SSM_INNER), lambda i: (0, 0)),
                  _const_spec(w.shape),
                  pl.BlockSpec((1, d), lambda i: (0, 0))],
        out_specs=pl.BlockSpec((tm, d), lambda i: (i, 0)),
        out_shape=jax.ShapeDtypeStruct((s, d), F32),
        compiler_params=_cparams(("parallel",), 56),
        name="ssm_out",
    )(x2, ya, yb, z, ng, w, gate)


def _rot_cols(w):
    half = MLA_ROPE // 2
    return jnp.concatenate([-w[..., half:], w[..., :half]], axis=-1)


def _pad_lanes(w, width):
    return jnp.pad(w, [(0, 0)] * (w.ndim - 1) + [(0, width - w.shape[-1])])


def kernel(x, c, positions, ada_w, ada_b, norm_mix_g, norm_ffn_g, ffn_w1, ffn_w2, hyb_w_in, sgu_norm_g, sgu_w, sgu_b, mla_q_norm_g, mla_kv_norm_g, mla_w_uq, mla_w_ukv, hyb_w_out, ssm_w_in, ssm_conv_w, ssm_conv_b, ssm_dt_bias, ssm_a_log, ssm_d, ssm_norm_g, ssm_w_out, final_norm_g):
    batch, s, d = x.shape
    assert batch == 1 and d == D_MODEL and s % 1024 == 0
    depth = ada_w.shape[0]
    assert depth == 2
    x2 = x.reshape(s, d)

    mod = _ada_mod(c, ada_w, ada_b)
    mods = [[mod[l, :, i * d:(i + 1) * d] for i in range(6)] for l in range(depth)]
    row = lambda v: v.reshape(1, -1)

    sh1, sc1, g1, sh2, sc2, g2 = mods[0]
    w_in = hyb_w_in[0]
    c_pe = 2 * SGU_WIDTH + MLA_Q_RANK + MLA_KV_RANK
    w_kpe = w_in[:, c_pe:]
    win_ext = jnp.concatenate(
        [w_in[:, :c_pe], _pad_lanes(w_kpe, V7X_LANES), _pad_lanes(_rot_cols(w_kpe), V7X_LANES)],
        axis=1).astype(BF16)
    wq = mla_w_uq[0].reshape(MLA_Q_RANK, MLA_HEADS, MLA_NOPE + MLA_ROPE)
    wqa = _pad_lanes(wq, QK_PAD).reshape(MLA_Q_RANK, MLA_HEADS * QK_PAD).astype(BF16)
    wqb = _pad_lanes(_rot_cols(wq[..., MLA_NOPE:]), V7X_LANES).reshape(
        MLA_Q_RANK, MLA_HEADS * V7X_LANES).astype(BF16)
    wkv = mla_w_ukv[0].reshape(MLA_KV_RANK, MLA_HEADS, MLA_NOPE + MLA_V)
    wk = wkv[..., :MLA_NOPE].reshape(MLA_KV_RANK, MLA_HEADS * MLA_NOPE).astype(BF16)
    wv = wkv[..., MLA_NOPE:].reshape(MLA_KV_RANK, MLA_HEADS * MLA_V).astype(BF16)
    sb_full = jnp.repeat(sgu_b[0].T, SGU_HEAD, axis=1)

    half = MLA_ROPE // 2
    inv_freq = ROPE_THETA ** (-jnp.arange(half, dtype=F32) / half)
    inv128 = jnp.tile(inv_freq, V7X_LANES // half).reshape(1, V7X_LANES)
    cos, sin = _rope_tables(positions.astype(F32).reshape(s, 1), inv128)
    q_scale = (MLA_NOPE + MLA_ROPE) ** -0.5 * LOG2E

    a_out, q, k, v = _hyb_in(x2, row(norm_mix_g[0]), sc1, sh1, win_ext, row(sgu_norm_g[0]),
                             sgu_w[0].astype(BF16), sb_full, row(mla_q_norm_g[0]),
                             row(mla_kv_norm_g[0]), wqa, wqb, wk, wv, cos, sin, q_scale)
    o = _flash(q, k, v)
    x2 = _hyb_out(x2, a_out, o, hyb_w_out[0].astype(BF16), g1)
    w1_bf = ffn_w1.astype(BF16)
    w2_bf = ffn_w2.astype(BF16)
    x2 = _ffn(x2, row(norm_ffn_g[0]), sc2, sh2, g2, row(final_norm_g), w1_bf, w2_bf, 0, False)

    sh1, sc1, g1, sh2, sc2, g2 = mods[1]
    z, xbc_t, dtt = _ssm_in(x2, row(norm_mix_g[1]), sc1, sh1, ssm_w_in[0].astype(BF16),
                            ssm_dt_bias[0].reshape(1, 2 * SSM_HEADS),
                            ssm_conv_w[0], row(ssm_conv_b[0]))
    alog_col = ssm_a_log[0].reshape(2 * SSM_HEADS, 1)
    d_full = jnp.repeat(ssm_d[0], SSM_HEAD_DIM).reshape(1, SSM_INNER)
    ya, yb = _ssd_scan(xbc_t, dtt, alog_col, d_full)
    x2 = _ssm_out(x2, ya, yb, z, row(ssm_norm_g[0]), ssm_w_out[0].astype(BF16), g1)
    x2 = _ffn(x2, row(norm_ffn_g[1]), sc2, sh2, g2, row(final_norm_g), w1_bf, w2_bf, 1, True)
    return x2.reshape(batch, s, d)
```

```python
import functools
import math

import jax
import jax.numpy as jnp
from jax import lax
from jax.experimental import pallas as pl
from jax.experimental.pallas import tpu as pltpu

F32 = jnp.float32
BF16 = jnp.bfloat16

V7X_LANES = 128
V7X_BF16_SUBLANE_TILE = 16
V7X_VMEM_BYTES = 64 * 1024 * 1024

D_MODEL = 2048
SGU_CHUNK = 128
SGU_GROUPS = 8
SGU_HEAD = 128
SGU_WIDTH = SGU_GROUPS * SGU_HEAD
MLA_HEADS = 8
MLA_Q_RANK = 512
MLA_KV_RANK = 512
MLA_NOPE = 128
MLA_ROPE = 64
MLA_V = 128
ROPE_THETA = 10000.0
HYB_MIX = SGU_WIDTH + MLA_HEADS * MLA_V
SSM_INNER = 2 * D_MODEL
SSM_HEAD_DIM = 64
SSM_HEADS = SSM_INNER // SSM_HEAD_DIM
SSM_GROUPS = 8
SSM_STATE = 128
SSM_CONV = 5
SSM_CHUNK = 256
SSM_CONV_CH = SSM_INNER + 2 * SSM_GROUPS * SSM_STATE
FFN_HIDDEN = 4 * D_MODEL
EPS = 1e-6

QK_PAD = 2 * V7X_LANES
HEADS_PER_GROUP = SSM_HEADS // SSM_GROUPS
PAIRS_PER_GROUP = HEADS_PER_GROUP * SSM_HEAD_DIM // V7X_LANES
X_BLOCKS = SSM_INNER // V7X_LANES
XBC_BLOCKS = SSM_CONV_CH // V7X_LANES
ZX_WIDTH = SSM_INNER + SSM_CONV_CH
LOG2E = 1.4426950408889634


def _cparams(sem, vmem_mib):
    return pltpu.CompilerParams(dimension_semantics=sem,
                                vmem_limit_bytes=vmem_mib * 1024 * 1024)


def _const_spec(shape):
    nd = len(shape)
    return pl.BlockSpec(shape, lambda *_: (0,) * nd, pipeline_mode=pl.Buffered(1))


def _norm_mod(x, g, sc, sh):
    y = x * lax.rsqrt(jnp.mean(x * x, axis=-1, keepdims=True) + EPS)
    return (y * g) * (1.0 + sc) + sh


def _silu(x):
    return x * jax.nn.sigmoid(x)


def _ada_kernel(c_ref, w_ref, b_ref, o_ref):
    c = c_ref[...]
    d = c.shape[-1]
    tn = w_ref.shape[2]
    cond_col = jnp.broadcast_to(_silu(c), (V7X_LANES, d)).T
    prod = w_ref[0] * jnp.tile(cond_col, (1, tn // V7X_LANES))
    o_ref[0] = jnp.sum(prod, axis=0, keepdims=True) + b_ref[0]


def _ada_mod(c, ada_w, ada_b):
    depth, d, n = ada_w.shape
    tn = 1024
    return pl.pallas_call(
        _ada_kernel,
        grid=(depth, n // tn),
        in_specs=[pl.BlockSpec((1, d), lambda l, j: (0, 0)),
                  pl.BlockSpec((1, d, tn), lambda l, j: (l, 0, j)),
                  pl.BlockSpec((1, 1, tn), lambda l, j: (l, 0, j))],
        out_specs=pl.BlockSpec((1, 1, tn), lambda l, j: (l, 0, j)),
        out_shape=jax.ShapeDtypeStruct((depth, 1, n), F32),
        compiler_params=_cparams(("parallel", "parallel"), 40),
        name="ada_mod",
    )(c, ada_w, ada_b.reshape(depth, 1, n))


def _rope_kernel(pos_ref, inv_ref, cos_ref, sin_ref):
    ang = pos_ref[...] * inv_ref[...]
    cos_ref[...] = jnp.cos(ang)
    sin_ref[...] = jnp.sin(ang)


def _rope_tables(pos_col, inv128):
    s = pos_col.shape[0]
    tm = min(s, 2048)
    return pl.pallas_call(
        _rope_kernel,
        grid=(s // tm,),
        in_specs=[pl.BlockSpec((tm, 1), lambda i: (i, 0)),
                  pl.BlockSpec((1, V7X_LANES), lambda i: (0, 0))],
        out_specs=[pl.BlockSpec((tm, V7X_LANES), lambda i: (i, 0))] * 2,
        out_shape=[jax.ShapeDtypeStruct((s, V7X_LANES), F32)] * 2,
        compiler_params=_cparams(("parallel",), 32),
        name="rope_tables",
    )(pos_col, inv128)


def _hyb_in_kernel(x_ref, g_ref, sc_ref, sh_ref, win_ref, lng_ref, sw_ref, sb_ref,
                   qg_ref, kvg_ref, wqa_ref, wqb_ref, wk_ref, wv_ref, cos_ref, sin_ref,
                   a_ref, q_ref, k_ref, v_ref, proj_scr, vn_scr, *, q_scale):
    tm = x_ref.shape[0]
    h = _norm_mod(x_ref[...], g_ref[...], sc_ref[...], sh_ref[...])
    proj_scr[...] = jnp.dot(h.astype(BF16), win_ref[...], preferred_element_type=F32)

    v = jax.nn.gelu(proj_scr[:, SGU_WIDTH:2 * SGU_WIDTH])
    vc = v - jnp.mean(v, axis=-1, keepdims=True)
    vn = vc * lax.rsqrt(jnp.mean(vc * vc, axis=-1, keepdims=True) + EPS) * lng_ref[...]
    vn_scr[...] = vn.astype(BF16)
    for c in range(tm // SGU_CHUNK):
        rows = slice(c * SGU_CHUNK, (c + 1) * SGU_CHUNK)
        for g in range(SGU_GROUPS):
            cols = slice(g * SGU_HEAD, (g + 1) * SGU_HEAD)
            mixed = jnp.dot(sw_ref[g], vn_scr[rows, cols], preferred_element_type=F32)
            u = jax.nn.gelu(proj_scr[rows, cols])
            a_ref[rows, cols] = (u * (mixed + sb_ref[:, cols])).astype(BF16)

    cos = cos_ref[...]
    sin = sin_ref[...]
    lat0 = 2 * SGU_WIDTH
    ql = proj_scr[:, lat0:lat0 + MLA_Q_RANK]
    qn = (ql * lax.rsqrt(jnp.mean(ql * ql, axis=-1, keepdims=True) + EPS) * qg_ref[...]).astype(BF16)
    qa = jnp.dot(qn, wqa_ref[...], preferred_element_type=F32)
    qb = jnp.dot(qn, wqb_ref[...], preferred_element_type=F32)
    for hd in range(MLA_HEADS):
        o = hd * QK_PAD
        q_ref[:, o:o + V7X_LANES] = (qa[:, o:o + V7X_LANES] * q_scale).astype(BF16)
        pe = (qa[:, o + V7X_LANES:o + QK_PAD] * cos
              + qb[:, hd * V7X_LANES:(hd + 1) * V7X_LANES] * sin)
        q_ref[:, o + V7X_LANES:o + QK_PAD] = (pe * q_scale).astype(BF16)

    kv0 = lat0 + MLA_Q_RANK
    kvl = proj_scr[:, kv0:kv0 + MLA_KV_RANK]
    kvn = (kvl * lax.rsqrt(jnp.mean(kvl * kvl, axis=-1, keepdims=True) + EPS) * kvg_ref[...]).astype(BF16)
    kn = jnp.dot(kvn, wk_ref[...], preferred_element_type=F32)
    vv = jnp.dot(kvn, wv_ref[...], preferred_element_type=F32)
    pe0 = kv0 + MLA_KV_RANK
    kpe = (proj_scr[:, pe0:pe0 + V7X_LANES] * cos
           + proj_scr[:, pe0 + V7X_LANES:pe0 + 2 * V7X_LANES] * sin).astype(BF16)
    ones = jnp.ones((tm, V7X_LANES), BF16)
    for hd in range(MLA_HEADS):
        o = hd * QK_PAD
        hs = slice(hd * V7X_LANES, (hd + 1) * V7X_LANES)
        k_ref[:, o:o + V7X_LANES] = kn[:, hs].astype(BF16)
        k_ref[:, o + V7X_LANES:o + QK_PAD] = kpe
        v_ref[:, o:o + V7X_LANES] = vv[:, hs].astype(BF16)
        v_ref[:, o + V7X_LANES:o + QK_PAD] = ones


def _hyb_in(x2, g, sc, sh, win, lng, sw, sb, qg, kvg, wqa, wqb, wk, wv, cos, sin, q_scale):
    s, d = x2.shape
    tm = min(s, 512)
    nproj = win.shape[1]
    row = lambda w: pl.BlockSpec((tm, w), lambda i: (i, 0))
    vec = lambda w: pl.BlockSpec((1, w), lambda i: (0, 0))
    hq = MLA_HEADS * QK_PAD
    return pl.pallas_call(
        functools.partial(_hyb_in_kernel, q_scale=q_scale),
        grid=(s // tm,),
        in_specs=[row(d), vec(d), vec(d), vec(d), _const_spec(win.shape), vec(SGU_WIDTH),
                  _const_spec(sw.shape), _const_spec(sb.shape), vec(MLA_Q_RANK), vec(MLA_KV_RANK),
                  _const_spec(wqa.shape), _const_spec(wqb.shape), _const_spec(wk.shape),
                  _const_spec(wv.shape), row(V7X_LANES), row(V7X_LANES)],
        out_specs=[row(SGU_WIDTH), row(hq), row(hq), row(hq)],
        out_shape=[jax.ShapeDtypeStruct((s, SGU_WIDTH), BF16),
                   jax.ShapeDtypeStruct((s, hq), BF16),
                   jax.ShapeDtypeStruct((s, hq), BF16),
                   jax.ShapeDtypeStruct((s, hq), BF16)],
        scratch_shapes=[pltpu.VMEM((tm, nproj), F32), pltpu.VMEM((tm, SGU_WIDTH), BF16)],
        compiler_params=_cparams(("parallel",), 56),
        name="hyb_in",
    )(x2, g, sc, sh, win, lng, sw, sb, qg, kvg, wqa, wqb, wk, wv, cos, sin)


def _flash_kernel(q_ref, k_ref, v_ref, o_ref, m_scr, acc_scr, *, tk, sub):
    nk = k_ref.shape[0] // tk
    tq = q_ref.shape[0]
    m_scr[...] = jnp.full(m_scr.shape, -jnp.inf, F32)
    acc_scr[...] = jnp.zeros(acc_scr.shape, F32)

    def body(j, carry):
        off = pl.multiple_of(j * tk, tk)
        k = k_ref[pl.ds(off, tk), :]
        v = v_ref[pl.ds(off, tk), :]
        for r in range(tq // sub):
            rows = slice(r * sub, (r + 1) * sub)
            s = lax.dot_general(q_ref[rows, :], k, (((1,), (1,)), ((), ())),
                                preferred_element_type=F32)
            m_prev = m_scr[rows, :]
            m_new = jnp.maximum(m_prev, jnp.max(s, axis=-1, keepdims=True))
            alpha = jnp.exp2(m_prev - m_new)
            p = jnp.exp2(s - jnp.tile(m_new, (1, tk // V7X_LANES)))
            acc_scr[rows, :] = (acc_scr[rows, :] * jnp.tile(alpha, (1, 2))
                                + jnp.dot(p.astype(BF16), v, preferred_element_type=F32))
            m_scr[rows, :] = m_new
        return carry

    lax.fori_loop(0, nk, body, 0, unroll=16 if nk % 16 == 0 else 1)
    acc = acc_scr[...]
    o_ref[...] = (acc[:, :MLA_V] / acc[:, MLA_V:]).astype(BF16)


def _flash(q, k, v):
    s = q.shape[0]
    tq = min(s, 2048)
    tk = min(s, 1024)
    return pl.pallas_call(
        functools.partial(_flash_kernel, tk=tk, sub=min(tq, 256)),
        grid=(MLA_HEADS, s // tq),
        in_specs=[pl.BlockSpec((tq, QK_PAD), lambda h, i: (i, h)),
                  pl.BlockSpec((s, QK_PAD), lambda h, i: (0, h)),
                  pl.BlockSpec((s, QK_PAD), lambda h, i: (0, h))],
        out_specs=pl.BlockSpec((tq, MLA_V), lambda h, i: (i, h)),
        out_shape=jax.ShapeDtypeStruct((s, MLA_HEADS * MLA_V), BF16),
        scratch_shapes=[pltpu.VMEM((tq, V7X_LANES), F32), pltpu.VMEM((tq, QK_PAD), F32)],
        compiler_params=_cparams(("parallel", "arbitrary"), 56),
        name="flash_attn",
    )(q, k, v)


def _hyb_out_kernel(x_ref, a_ref, o_ref, w_ref, gate_ref, y_ref):
    m = (jnp.dot(a_ref[...], w_ref[:SGU_WIDTH, :], preferred_element_type=F32)
         + jnp.dot(o_ref[...], w_ref[SGU_WIDTH:, :], preferred_element_type=F32))
    y_ref[...] = x_ref[...] + gate_ref[...] * m


def _hyb_out(x2, a, o, w, gate):
    s, d = x2.shape
    tm = min(s, 512)
    return pl.pallas_call(
        _hyb_out_kernel,
        grid=(s // tm,),
        in_specs=[pl.BlockSpec((tm, d), lambda i: (i, 0)),
                  pl.BlockSpec((tm, a.shape[1]), lambda i: (i, 0)),
                  pl.BlockSpec((tm, o.shape[1]), lambda i: (i, 0)),
                  _const_spec(w.shape),
                  pl.BlockSpec((1, d), lambda i: (0, 0))],
        out_specs=pl.BlockSpec((tm, d), lambda i: (i, 0)),
        out_shape=jax.ShapeDtypeStruct((s, d), F32),
        compiler_params=_cparams(("parallel",), 48),
        name="hyb_out",
    )(x2, a, o, w, gate)


def _ffn_kernel(x_ref, g_ref, sc_ref, sh_ref, gate_ref, fg_ref, w1_ref, w2_ref, o_ref,
                h_scr, *, final_norm):
    k = pl.program_id(1)

    @pl.when(k == 0)
    def _():
        h_scr[...] = _norm_mod(x_ref[...], g_ref[...], sc_ref[...], sh_ref[...]).astype(BF16)
        o_ref[...] = jnp.zeros(o_ref.shape, F32)

    hid = jnp.dot(h_scr[...], w1_ref[...], preferred_element_type=F32)
    hid = jnp.square(jnp.maximum(hid, 0.0)).astype(BF16)
    o_ref[...] += jnp.dot(hid, w2_ref[...], preferred_element_type=F32)

    @pl.when(k == pl.num_programs(1) - 1)
    def _():
        y = x_ref[...] + gate_ref[...] * o_ref[...]
        if final_norm:
            y = y * lax.rsqrt(jnp.mean(y * y, axis=-1, keepdims=True) + EPS) * fg_ref[...]
        o_ref[...] = y


def _ffn(x2, g, sc, sh, gate, fg, w1, w2, layer, final_norm):
    s, d = x2.shape
    hidden = w1.shape[2]
    tm = min(s, 512)
    tk = 2048
    vec = pl.BlockSpec((1, d), lambda i, k: (0, 0))
    return pl.pallas_call(
        functools.partial(_ffn_kernel, final_norm=final_norm),
        grid=(s // tm, hidden // tk),
        in_specs=[pl.BlockSpec((tm, d), lambda i, k: (i, 0)), vec, vec, vec, vec, vec,
                  pl.BlockSpec((None, d, tk), lambda i, k: (layer, 0, k)),
                  pl.BlockSpec((None, tk, d), lambda i, k: (layer, k, 0))],
        out_specs=pl.BlockSpec((tm, d), lambda i, k: (i, 0)),
        out_shape=jax.ShapeDtypeStruct((s, d), F32),
        scratch_shapes=[pltpu.VMEM((tm, d), BF16)],
        compiler_params=_cparams(("parallel", "arbitrary"), 60),
        name="ffn",
    )(x2, g, sc, sh, gate, fg, w1, w2)


SSM_IN_HALO = V7X_BF16_SUBLANE_TILE
SSM_IN_SUB = 256
SSM_IN_ROWS = 1024


def _ssm_in_kernel(xp_ref, x_ref, xn_ref, g_ref, sc_ref, sh_ref, w_ref, wdt_ref, dtb_ref,
                   cw_ref, cb_ref, z_ref, xbc_ref, dtt_ref, h_scr, *, nz):
    i = pl.program_id(0)
    j = pl.program_id(1)
    tm = x_ref.shape[0]
    halo = xp_ref.shape[0]
    pad = SSM_CONV // 2

    @pl.when(j == 0)
    def _():
        norm = lambda v: _norm_mod(v, g_ref[...], sc_ref[...], sh_ref[...])
        hb = norm(x_ref[...]).astype(BF16)
        h_scr[0:halo, :] = jnp.where(i == 0, 0.0, norm(xp_ref[...])).astype(BF16)
        h_scr[halo:halo + tm, :] = hb
        h_scr[halo + tm:, :] = jnp.where(i == pl.num_programs(0) - 1, 0.0, norm(xn_ref[...])).astype(BF16)
        r = jnp.dot(hb, wdt_ref[...], preferred_element_type=F32) + dtb_ref[...]
        dtt_ref[...] = (jnp.maximum(r, 0.0) + jnp.log1p(jnp.exp(-jnp.abs(r)))).T

    @pl.when(j < nz)
    def _():
        z_ref[...] = jnp.dot(h_scr[halo:halo + tm, :], w_ref[...],
                             preferred_element_type=F32).astype(BF16)

    @pl.when(j >= nz)
    def _():
        rb = min(tm, SSM_IN_ROWS)
        for r in range(tm // rb):
            for c in range(w_ref.shape[1] // SSM_IN_SUB):
                cols = slice(c * SSM_IN_SUB, (c + 1) * SSM_IN_SUB)
                pr = jnp.dot(h_scr[r * rb:r * rb + rb + 2 * halo, :], w_ref[:, cols],
                             preferred_element_type=F32)
                acc = jnp.broadcast_to(cb_ref[:, cols], (rb, SSM_IN_SUB))
                for t in range(SSM_CONV):
                    acc = acc + cw_ref[t:t + 1, cols] * pr[halo - pad + t:halo - pad + t + rb, :]
                y = _silu(acc)
                for u in range(SSM_IN_SUB // V7X_LANES):
                    xbc_ref[c * (SSM_IN_SUB // V7X_LANES) + u, r * rb:(r + 1) * rb, :] = (
                        y[:, u * V7X_LANES:(u + 1) * V7X_LANES].astype(BF16))


def _ssm_in(x2, g, sc, sh, w_in, dt_bias, conv_w, conv_b):
    s, d = x2.shape
    ndt = 2 * SSM_HEADS
    tm = min(s, 1024)
    tn = 1024
    halo = SSM_IN_HALO
    nz = SSM_INNER // tn
    nrow_h = s // halo
    vec = pl.BlockSpec((1, d), lambda i, j: (0, 0))
    ccol = lambda j: jnp.maximum(j - nz, 0)
    return pl.pallas_call(
        functools.partial(_ssm_in_kernel, nz=nz),
        grid=(s // tm, ZX_WIDTH // tn),
        in_specs=[pl.BlockSpec((halo, d), lambda i, j: (jnp.maximum(i * (tm // halo) - 1, 0), 0)),
                  pl.BlockSpec((tm, d), lambda i, j: (i, 0)),
                  pl.BlockSpec((halo, d), lambda i, j: (jnp.minimum((i + 1) * (tm // halo), nrow_h - 1), 0)),
                  vec, vec, vec,
                  pl.BlockSpec((d, tn), lambda i, j: (0, j)),
                  pl.BlockSpec((d, ndt), lambda i, j: (0, ZX_WIDTH // ndt)),
                  pl.BlockSpec((1, ndt), lambda i, j: (0, 0)),
                  pl.BlockSpec((SSM_CONV, tn), lambda i, j: (0, ccol(j))),
                  pl.BlockSpec((1, tn), lambda i, j: (0, ccol(j)))],
        out_specs=[pl.BlockSpec((tm, tn), lambda i, j: (i, jnp.minimum(j, nz - 1))),
                   pl.BlockSpec((tn // V7X_LANES, tm, V7X_LANES), lambda i, j: (ccol(j), i, 0)),
                   pl.BlockSpec((ndt, tm), lambda i, j: (0, i))],
        out_shape=[jax.ShapeDtypeStruct((s, SSM_INNER), BF16),
                   jax.ShapeDtypeStruct((XBC_BLOCKS, s, V7X_LANES), BF16),
                   jax.ShapeDtypeStruct((ndt, s), F32)],
        scratch_shapes=[pltpu.VMEM((tm + 2 * halo, d), BF16)],
        compiler_params=_cparams(("parallel", "arbitrary"), 48),
        name="ssm_in",
    )(x2, x2, x2, g, sc, sh, w_in, w_in, dt_bias, conv_w, conv_b)


def _chunk_cumsum(dta, tri):
    n = dta.shape[0]
    hi = dta.astype(BF16)
    r1 = dta - hi.astype(F32)
    mid = r1.astype(BF16)
    lo = (r1 - mid.astype(F32)).astype(BF16)
    cs = jnp.dot(jnp.concatenate([hi, mid, lo], axis=0), tri, preferred_element_type=F32)
    return cs[0:n] + cs[n:2 * n] + cs[2 * n:3 * n]


def _lane_bcast_col(row, L):
    return jnp.broadcast_to(row, (V7X_LANES, L)).T


def _ssd_kernel(xa_ref, ba_ref, ca_ref, dta_ref, xb_ref, bb_ref, cb_ref, dtb_ref, alog_ref, d_ref,
                ya_ref, yb_ref, stf_scr, stb_scr):
    L = xa_ref.shape[1]
    P = SSM_HEAD_DIM
    H = SSM_HEADS

    @pl.when(pl.program_id(0) == 0)
    def _():
        stf_scr[...] = jnp.zeros(stf_scr.shape, F32)
        stb_scr[...] = jnp.zeros(stb_scr.shape, F32)

    neg_a = -jnp.exp(alog_ref[...])
    dt_a = dta_ref[...]
    dt_b = dtb_ref[H:, :]
    kk = lax.broadcasted_iota(jnp.int32, (L, L), 0)
    ii = lax.broadcasted_iota(jnp.int32, (L, L), 1)
    tri_f = jnp.where(kk <= ii, 1.0, 0.0).astype(BF16)
    tri_b = jnp.where(kk >= ii, 1.0, 0.0).astype(BF16)
    cf = _chunk_cumsum(dt_a[:H] * neg_a[:H], tri_f) * LOG2E
    rc = _chunk_cumsum(jnp.concatenate([dt_a[H:] * neg_a[H:], dt_b * neg_a[H:]], axis=0), tri_b) * LOG2E
    ra = rc[:H]
    rb = rc[H:]
    cf_dt = cf - jnp.log(dt_a[:H]) * LOG2E
    ra_dt = ra - jnp.log(dt_a[H:]) * LOG2E
    rb_dt = rb - jnp.log(dt_b) * LOG2E
    f_end = cf[:, L - 1:L]
    b_end = rb[:, 0:1]
    to_end_f = jnp.exp2(f_end - cf_dt)
    to_end_b = jnp.exp2(b_end - rb_dt)
    dec_f = jnp.exp2(f_end)
    dec_b = jnp.exp2(b_end)
    below = kk > ii
    diag = kk == ii
    dt_sum = jnp.log(dt_a[:H] + dt_a[H:]) * LOG2E
    low =lax.broadcasted_iota(jnp.int32, (1, V7X_LANES), 1) < P
    heads_per_block = V7X_LANES // P
    reps = (1, L // V7X_LANES)

    for g in range(SSM_GROUPS):
        gl = slice(g * HEADS_PER_GROUP * P, (g + 1) * HEADS_PER_GROUP * P)
        cga = ca_ref[g]
        bta = ba_ref[g].astype(F32).T.astype(BF16)
        cba = jnp.dot(cga, bta, preferred_element_type=F32).astype(BF16)
        stf = stf_scr[:, gl]
        ysf = jnp.dot(cga, stf.astype(BF16), preferred_element_type=F32)
        cgb = cb_ref[g]
        btb = bb_ref[g].astype(F32).T.astype(BF16)
        stb = stb_scr[:, gl]
        ysb = jnp.dot(cgb, stb.astype(BF16), preferred_element_type=F32)
        for q in range(PAIRS_PER_GROUP):
            blk = g * PAIRS_PER_GROUP + q
            xpa = xa_ref[blk]
            xpb = xb_ref[blk]
            lanes = slice(q * V7X_LANES, (q + 1) * V7X_LANES)
            yas, ybs, sfs, sbs = [], [], [], []
            for e in range(heads_per_block):
                hd = blk * heads_per_block + e
                hr = slice(hd, hd + 1)
                cf_col = _lane_bcast_col(cf[hr, :], L)
                ra_col = _lane_bcast_col(ra[hr, :], L)
                rb_col = _lane_bcast_col(rb[hr, :], L)
                arg = jnp.where(below, jnp.tile(cf_col, reps) - cf_dt[hr, :],
                                jnp.where(diag, dt_sum[hr, :], jnp.tile(ra_col, reps) - ra_dt[hr, :]))
                w = jnp.exp2(arg).astype(BF16) * cba
                yh = jnp.dot(w, xpa, preferred_element_type=F32)
                yas.append(yh + ysf[:, lanes] * jnp.exp2(cf_col))
                ybs.append(ysb[:, lanes] * jnp.exp2(rb_col))
                sfs.append(jnp.dot(bta * to_end_f[hr, :].astype(BF16), xpa, preferred_element_type=F32))
                sbs.append(jnp.dot(btb * to_end_b[hr, :].astype(BF16), xpb, preferred_element_type=F32))
            sl = slice(blk * V7X_LANES, (blk + 1) * V7X_LANES)
            ya = jnp.where(low, yas[0], yas[1]) + d_ref[:, sl] * xpa.astype(F32)
            ya_ref[blk] = ya.astype(BF16)
            yb_ref[blk] = jnp.where(low, ybs[0], ybs[1]).astype(BF16)
            h0 = blk * heads_per_block
            decf = jnp.where(low, dec_f[h0:h0 + 1, :], dec_f[h0 + 1:h0 + 2, :])
            decb = jnp.where(low, dec_b[h0:h0 + 1, :], dec_b[h0 + 1:h0 + 2, :])
            stf_scr[:, sl] = stf[:, lanes] * decf + jnp.where(low, sfs[0], sfs[1])
            stb_scr[:, sl] = stb[:, lanes] * decb + jnp.where(low, sbs[0], sbs[1])


def _ssd_scan(xbc_t, dtt, alog_col, d_full):
    s = xbc_t.shape[1]
    L = SSM_CHUNK
    nc = s // L
    fwd = lambda t: t
    bwd = lambda t: nc - 1 - t
    b0 = X_BLOCKS // SSM_GROUPS
    c0 = (X_BLOCKS + SSM_GROUPS) // SSM_GROUPS

    def chunk_specs(cidx):
        return [pl.BlockSpec((X_BLOCKS, L, V7X_LANES), lambda t: (0, cidx(t), 0)),
                pl.BlockSpec((SSM_GROUPS, L, V7X_LANES), lambda t: (b0, cidx(t), 0)),
                pl.BlockSpec((SSM_GROUPS, L, V7X_LANES), lambda t: (c0, cidx(t), 0)),
                pl.BlockSpec((2 * SSM_HEADS, L), lambda t: (0, cidx(t)))]

    yspec = lambda cidx: pl.BlockSpec((X_BLOCKS, L, V7X_LANES), lambda t: (0, cidx(t), 0))
    yshape = jax.ShapeDtypeStruct((X_BLOCKS, s, V7X_LANES), BF16)
    return pl.pallas_call(
        _ssd_kernel,
        grid=(nc,),
        in_specs=chunk_specs(fwd) + chunk_specs(bwd)
        + [pl.BlockSpec((2 * SSM_HEADS, 1), lambda t: (0, 0)),
           pl.BlockSpec((1, SSM_INNER), lambda t: (0, 0))],
        out_specs=[yspec(fwd), yspec(bwd)],
        out_shape=[yshape, yshape],
        scratch_shapes=[pltpu.VMEM((SSM_STATE, SSM_INNER), F32), pltpu.VMEM((SSM_STATE, SSM_INNER), F32)],
        compiler_params=_cparams(("arbitrary",), 48),
        name="ssd_bidir",
    )(xbc_t, xbc_t, xbc_t, dtt, xbc_t, xbc_t, xbc_t, dtt, alog_col, d_full)


def _ssm_gate_norm(g, ya_ref, yb_ref, z_ref, ng_ref):
    per_group = PAIRS_PER_GROUP
    gw = SSM_INNER // SSM_GROUPS
    ys = []
    ss = None
    for p in range(per_group):
        j = g * per_group + p
        lanes = slice(j * V7X_LANES, (j + 1) * V7X_LANES)
        y = ya_ref[j].astype(F32) + yb_ref[j].astype(F32)
        y = y * _silu(z_ref[:, lanes].astype(F32))
        ys.append(y)
        t = jnp.sum(y * y, axis=-1, keepdims=True)
        ss = t if ss is None else ss + t
    scale = lax.rsqrt(ss * (1.0 / gw) + EPS)
    pieces = []
    for p in range(per_group):
        j = g * per_group + p
        lanes = slice(j * V7X_LANES, (j + 1) * V7X_LANES)
        pieces.append((ys[p] * scale * ng_ref[:, lanes]).astype(BF16))
    return jnp.concatenate(pieces, axis=1)


def _ssm_out_kernel(x_ref, ya_ref, yb_ref, z_ref, ng_ref, w_ref, gate_ref, o_ref):
    gw = SSM_INNER // SSM_GROUPS
    acc = None
    for g in range(SSM_GROUPS):
        yn = _ssm_gate_norm(g, ya_ref, yb_ref, z_ref, ng_ref)
        part = jnp.dot(yn, w_ref[g * gw:(g + 1) * gw, :], preferred_element_type=F32)
        acc = part if acc is None else acc + part
    o_ref[...] = x_ref[...] + gate_ref[...] * acc


def _ssm_out(x2, ya, yb, z, ng, w, gate):
    s, d = x2.shape
    tm = min(s, 256)
    yspec = pl.BlockSpec((X_BLOCKS, tm, V7X_LANES), lambda i: (0, i, 0))
    return pl.pallas_call(
        _ssm_out_kernel,
        grid=(s // tm,),
        in_specs=[pl.BlockSpec((tm, d), lambda i: (i, 0)), yspec, yspec,
                  pl.BlockSpec((tm, SSM_INNER), lambda i: (i, 0)),
                  pl.BlockSpec((1, SSM_INNER), lambda i: (0, 0)),
                  _const_spec(w.shape),
                  pl.BlockSpec((1, d), lambda i: (0, 0))],
        out_specs=pl.BlockSpec((tm, d), lambda i: (i, 0)),
        out_shape=jax.ShapeDtypeStruct((s, d), F32),
        compiler_params=_cparams(("parallel",), 56),
        name="ssm_out",
    )(x2, ya, yb, z, ng, w, gate)


def _rot_cols(w):
    half = MLA_ROPE // 2
    return jnp.concatenate([-w[..., half:], w[..., :half]], axis=-1)


def _pad_lanes(w, width):
    return jnp.pad(w, [(0, 0)] * (w.ndim - 1) + [(0, width - w.shape[-1])])


def kernel(x, c, positions, ada_w, ada_b, norm_mix_g, norm_ffn_g, ffn_w1, ffn_w2, hyb_w_in, sgu_norm_g, sgu_w, sgu_b, mla_q_norm_g, mla_kv_norm_g, mla_w_uq, mla_w_ukv, hyb_w_out, ssm_w_in, ssm_conv_w, ssm_conv_b, ssm_dt_bias, ssm_a_log, ssm_d, ssm_norm_g, ssm_w_out, final_norm_g):
    batch, s, d = x.shape
    assert batch == 1 and d == D_MODEL and s % 1024 == 0
    depth = ada_w.shape[0]
    assert depth == 2
    x2 = x.reshape(s, d)

    mod = _ada_mod(c, ada_w, ada_b)
    mods = [[mod[l, :, i * d:(i + 1) * d] for i in range(6)] for l in range(depth)]
    row = lambda v: v.reshape(1, -1)

    sh1, sc1, g1, sh2, sc2, g2 = mods[0]
    w_in = hyb_w_in[0]
    c_pe = 2 * SGU_WIDTH + MLA_Q_RANK + MLA_KV_RANK
    w_kpe = w_in[:, c_pe:]
    win_ext = jnp.concatenate(
        [w_in[:, :c_pe], _pad_lanes(w_kpe, V7X_LANES), _pad_lanes(_rot_cols(w_kpe), V7X_LANES)],
        axis=1).astype(BF16)
    wq = mla_w_uq[0].reshape(MLA_Q_RANK, MLA_HEADS, MLA_NOPE + MLA_ROPE)
    wqa = _pad_lanes(wq, QK_PAD).reshape(MLA_Q_RANK, MLA_HEADS * QK_PAD).astype(BF16)
    wqb = _pad_lanes(_rot_cols(wq[..., MLA_NOPE:]), V7X_LANES).reshape(
        MLA_Q_RANK, MLA_HEADS * V7X_LANES).astype(BF16)
    wkv = mla_w_ukv[0].reshape(MLA_KV_RANK, MLA_HEADS, MLA_NOPE + MLA_V)
    wk = wkv[..., :MLA_NOPE].reshape(MLA_KV_RANK, MLA_HEADS * MLA_NOPE).astype(BF16)
    wv = wkv[..., MLA_NOPE:].reshape(MLA_KV_RANK, MLA_HEADS * MLA_V).astype(BF16)
    sb_full = jnp.repeat(sgu_b[0].T, SGU_HEAD, axis=1)

    half = MLA_ROPE // 2
    inv_freq = ROPE_THETA ** (-jnp.arange(half, dtype=F32) / half)
    inv128 = jnp.tile(inv_freq, V7X_LANES // half).reshape(1, V7X_LANES)
    cos, sin = _rope_tables(positions.astype(F32).reshape(s, 1), inv128)
    q_scale = (MLA_NOPE + MLA_ROPE) ** -0.5 * LOG2E

    a_out, q, k, v = _hyb_in(x2, row(norm_mix_g[0]), sc1, sh1, win_ext, row(sgu_norm_g[0]),
                             sgu_w[0].astype(BF16), sb_full, row(mla_q_norm_g[0]),
                             row(mla_kv_norm_g[0]), wqa, wqb, wk, wv, cos, sin, q_scale)
    o = _flash(q, k, v)
    x2 = _hyb_out(x2, a_out, o, hyb_w_out[0].astype(BF16), g1)
    w1_bf = ffn_w1.astype(BF16)
    w2_bf = ffn_w2.astype(BF16)
    x2 = _ffn(x2, row(norm_ffn_g[0]), sc2, sh2, g2, row(final_norm_g), w1_bf, w2_bf, 0, False)

    sh1, sc1, g1, sh2, sc2, g2 = mods[1]
    z, xbc_t, dtt = _ssm_in(x2, row(norm_mix_g[1]), sc1, sh1, ssm_w_in[0].astype(BF16),
                            ssm_dt_bias[0].reshape(1, 2 * SSM_HEADS),
                            ssm_conv_w[0], row(ssm_conv_b[0]))
    alog_col = ssm_a_log[0].reshape(2 * SSM_HEADS, 1)
    d_full = jnp.repeat(ssm_d[0], SSM_HEAD_DIM).reshape(1, SSM_INNER)
    ya, yb = _ssd_scan(xbc_t, dtt, alog_col, d_full)
    x2 = _ssm_out(x2, ya, yb, z, row(ssm_norm_g[0]), ssm_w_out[0].astype(BF16), g1)
    x2 = _ffn(x2, row(norm_ffn_g[1]), sc2, sh2, g2, row(final_norm_g), w1_bf, w2_bf, 1, True)
    return x2.reshape(batch, s, d)
```

```python
import functools
import math

import jax
import jax.numpy as jnp
from jax import lax
from jax.experimental import pallas as pl
from jax.experimental.pallas import tpu as pltpu

F32 = jnp.float32
BF16 = jnp.bfloat16

V7X_LANES = 128
V7X_BF16_SUBLANE_TILE = 16
V7X_VMEM_BYTES = 64 * 1024 * 1024

D_MODEL = 2048
SGU_CHUNK = 128
SGU_GROUPS = 8
SGU_HEAD = 128
SGU_WIDTH = SGU_GROUPS * SGU_HEAD
MLA_HEADS = 8
MLA_Q_RANK = 512
MLA_KV_RANK = 512
MLA_NOPE = 128
MLA_ROPE = 64
MLA_V = 128
ROPE_THETA = 10000.0
HYB_MIX = SGU_WIDTH + MLA_HEADS * MLA_V
SSM_INNER = 2 * D_MODEL
SSM_HEAD_DIM = 64
SSM_HEADS = SSM_INNER // SSM_HEAD_DIM
SSM_GROUPS = 8
SSM_STATE = 128
SSM_CONV = 5
SSM_CHUNK = 256
SSM_CONV_CH = SSM_INNER + 2 * SSM_GROUPS * SSM_STATE
FFN_HIDDEN = 4 * D_MODEL
EPS = 1e-6

QK_PAD = 2 * V7X_LANES
HEADS_PER_GROUP = SSM_HEADS // SSM_GROUPS
PAIRS_PER_GROUP = HEADS_PER_GROUP * SSM_HEAD_DIM // V7X_LANES
X_BLOCKS = SSM_INNER // V7X_LANES
XBC_BLOCKS = SSM_CONV_CH // V7X_LANES
ZX_WIDTH = SSM_INNER + SSM_CONV_CH
LOG2E = 1.4426950408889634


def _cparams(sem, vmem_mib):
    return pltpu.CompilerParams(dimension_semantics=sem,
                                vmem_limit_bytes=vmem_mib * 1024 * 1024)


def _const_spec(shape):
    nd = len(shape)
    return pl.BlockSpec(shape, lambda *_: (0,) * nd, pipeline_mode=pl.Buffered(1))


def _norm_mod(x, g, sc, sh):
    y = x * lax.rsqrt(jnp.mean(x * x, axis=-1, keepdims=True) + EPS)
    return (y * g) * (1.0 + sc) + sh


def _silu(x):
    return x * jax.nn.sigmoid(x)


def _ada_kernel(c_ref, w_ref, b_ref, o_ref):
    c = c_ref[...]
    d = c.shape[-1]
    tn = w_ref.shape[2]
    cond_col = jnp.broadcast_to(_silu(c), (V7X_LANES, d)).T
    prod = w_ref[0] * jnp.tile(cond_col, (1, tn // V7X_LANES))
    o_ref[0] = jnp.sum(prod, axis=0, keepdims=True) + b_ref[0]


def _ada_mod(c, ada_w, ada_b):
    depth, d, n = ada_w.shape
    tn = 1024
    return pl.pallas_call(
        _ada_kernel,
        grid=(depth, n // tn),
        in_specs=[pl.BlockSpec((1, d), lambda l, j: (0, 0)),
                  pl.BlockSpec((1, d, tn), lambda l, j: (l, 0, j)),
                  pl.BlockSpec((1, 1, tn), lambda l, j: (l, 0, j))],
        out_specs=pl.BlockSpec((1, 1, tn), lambda l, j: (l, 0, j)),
        out_shape=jax.ShapeDtypeStruct((depth, 1, n), F32),
        compiler_params=_cparams(("parallel", "parallel"), 40),
        name="ada_mod",
    )(c, ada_w, ada_b.reshape(depth, 1, n))


def _rope_kernel(pos_ref, inv_ref, cos_ref, sin_ref):
    ang = pos_ref[...] * inv_ref[...]
    cos_ref[...] = jnp.cos(ang)
    sin_ref[...] = jnp.sin(ang)


def _rope_tables(pos_col, inv128):
    s = pos_col.shape[0]
    tm = min(s, 2048)
    return pl.pallas_call(
        _rope_kernel,
        grid=(s // tm,),
        in_specs=[pl.BlockSpec((tm, 1), lambda i: (i, 0)),
                  pl.BlockSpec((1, V7X_LANES), lambda i: (0, 0))],
        out_specs=[pl.BlockSpec((tm, V7X_LANES), lambda i: (i, 0))] * 2,
        out_shape=[jax.ShapeDtypeStruct((s, V7X_LANES), F32)] * 2,
        compiler_params=_cparams(("parallel",), 32),
        name="rope_tables",
    )(pos_col, inv128)


def _hyb_in_kernel(x_ref, g_ref, sc_ref, sh_ref, win_ref, lng_ref, sw_ref, sb_ref,
                   qg_ref, kvg_ref, wqa_ref, wqb_ref, wk_ref, wv_ref, cos_ref, sin_ref,
                   a_ref, q_ref, k_ref, v_ref, proj_scr, vn_scr, *, q_scale):
    tm = x_ref.shape[0]
    h = _norm_mod(x_ref[...], g_ref[...], sc_ref[...], sh_ref[...])
    proj_scr[...] = jnp.dot(h.astype(BF16), win_ref[...], preferred_element_type=F32)

    v = jax.nn.gelu(proj_scr[:, SGU_WIDTH:2 * SGU_WIDTH])
    vc = v - jnp.mean(v, axis=-1, keepdims=True)
    vn = vc * lax.rsqrt(jnp.mean(vc * vc, axis=-1, keepdims=True) + EPS) * lng_ref[...]
    vn_scr[...] = vn.astype(BF16)
    for c in range(tm // SGU_CHUNK):
        rows = slice(c * SGU_CHUNK, (c + 1) * SGU_CHUNK)
        for g in range(SGU_GROUPS):
            cols = slice(g * SGU_HEAD, (g + 1) * SGU_HEAD)
            mixed = jnp.dot(sw_ref[g], vn_scr[rows, cols], preferred_element_type=F32)
            u = jax.nn.gelu(proj_scr[rows, cols])
            a_ref[rows, cols] = (u * (mixed + sb_ref[:, cols])).astype(BF16)

    cos = cos_ref[...]
    sin = sin_ref[...]
    lat0 = 2 * SGU_WIDTH
    ql = proj_scr[:, lat0:lat0 + MLA_Q_RANK]
    qn = (ql * lax.rsqrt(jnp.mean(ql * ql, axis=-1, keepdims=True) + EPS) * qg_ref[...]).astype(BF16)
    qa = jnp.dot(qn, wqa_ref[...], preferred_element_type=F32)
    qb = jnp.dot(qn, wqb_ref[...], preferred_element_type=F32)
    for hd in range(MLA_HEADS):
        o = hd * QK_PAD
        q_ref[:, o:o + V7X_LANES] = (qa[:, o:o + V7X_LANES] * q_scale).astype(BF16)
        pe = (qa[:, o + V7X_LANES:o + QK_PAD] * cos
              + qb[:, hd * V7X_LANES:(hd + 1) * V7X_LANES] * sin)
        q_ref[:, o + V7X_LANES:o + QK_PAD] = (pe * q_scale).astype(BF16)

    kv0 = lat0 + MLA_Q_RANK
    kvl = proj_scr[:, kv0:kv0 + MLA_KV_RANK]
    kvn = (kvl * lax.rsqrt(jnp.mean(kvl * kvl, axis=-1, keepdims=True) + EPS) * kvg_ref[...]).astype(BF16)
    kn = jnp.dot(kvn, wk_ref[...], preferred_element_type=F32)
    vv = jnp.dot(kvn, wv_ref[...], preferred_element_type=F32)
    pe0 = kv0 + MLA_KV_RANK
    kpe = (proj_scr[:, pe0:pe0 + V7X_LANES] * cos
           + proj_scr[:, pe0 + V7X_LANES:pe0 + 2 * V7X_LANES] * sin).astype(BF16)
    ones = jnp.ones((tm, V7X_LANES), BF16)
    for hd in range(MLA_HEADS):
        o = hd * QK_PAD
        hs = slice(hd * V7X_LANES, (hd + 1) * V7X_LANES)
        k_ref[:, o:o + V7X_LANES] = kn[:, hs].astype(BF16)
        k_ref[:, o + V7X_LANES:o + QK_PAD] = kpe
        v_ref[:, o:o + V7X_LANES] = vv[:, hs].astype(BF16)
        v_ref[:, o + V7X_LANES:o + QK_PAD] = ones


def _hyb_in(x2, g, sc, sh, win, lng, sw, sb, qg, kvg, wqa, wqb, wk, wv, cos, sin, q_scale):
    s, d = x2.shape
    tm = min(s, 512)
    nproj = win.shape[1]
    row = lambda w: pl.BlockSpec((tm, w), lambda i: (i, 0))
    vec = lambda w: pl.BlockSpec((1, w), lambda i: (0, 0))
    hq = MLA_HEADS * QK_PAD
    return pl.pallas_call(
        functools.partial(_hyb_in_kernel, q_scale=q_scale),
        grid=(s // tm,),
        in_specs=[row(d), vec(d), vec(d), vec(d), _const_spec(win.shape), vec(SGU_WIDTH),
                  _const_spec(sw.shape), _const_spec(sb.shape), vec(MLA_Q_RANK), vec(MLA_KV_RANK),
                  _const_spec(wqa.shape), _const_spec(wqb.shape), _const_spec(wk.shape),
                  _const_spec(wv.shape), row(V7X_LANES), row(V7X_LANES)],
        out_specs=[row(SGU_WIDTH), row(hq), row(hq), row(hq)],
        out_shape=[jax.ShapeDtypeStruct((s, SGU_WIDTH), BF16),
                   jax.ShapeDtypeStruct((s, hq), BF16),
                   jax.ShapeDtypeStruct((s, hq), BF16),
                   jax.ShapeDtypeStruct((s, hq), BF16)],
        scratch_shapes=[pltpu.VMEM((tm, nproj), F32), pltpu.VMEM((tm, SGU_WIDTH), BF16)],
        compiler_params=_cparams(("parallel",), 56),
        name="hyb_in",
    )(x2, g, sc, sh, win, lng, sw, sb, qg, kvg, wqa, wqb, wk, wv, cos, sin)


def _flash_kernel(q_ref, k_ref, v_ref, o_ref, m_scr, acc_scr, *, tk, sub):
    nk = k_ref.shape[0] // tk
    tq = q_ref.shape[0]
    m_scr[...] = jnp.full(m_scr.shape, -jnp.inf, F32)
    acc_scr[...] = jnp.zeros(acc_scr.shape, F32)

    def body(j, carry):
        off = pl.multiple_of(j * tk, tk)
        k = k_ref[pl.ds(off, tk), :]
        v = v_ref[pl.ds(off, tk), :]
        for r in range(tq // sub):
            rows = slice(r * sub, (r + 1) * sub)
            s = lax.dot_general(q_ref[rows, :], k, (((1,), (1,)), ((), ())),
                                preferred_element_type=F32)
            m_prev = m_scr[rows, :]
            m_new = jnp.maximum(m_prev, jnp.max(s, axis=-1, keepdims=True))
            alpha = jnp.exp2(m_prev - m_new)
            p = jnp.exp2(s - jnp.tile(m_new, (1, tk // V7X_LANES)))
            acc_scr[rows, :] = (acc_scr[rows, :] * jnp.tile(alpha, (1, 2))
                                + jnp.dot(p.astype(BF16), v, preferred_element_type=F32))
            m_scr[rows, :] = m_new
        return carry

    lax.fori_loop(0, nk, body, 0, unroll=8 if nk % 8 == 0 else 1)
    acc = acc_scr[...]
    o_ref[...] = (acc[:, :MLA_V] / acc[:, MLA_V:]).astype(BF16)


def _flash(q, k, v):
    s = q.shape[0]
    tq = min(s, 2048)
    tk = min(s, 1024)
    return pl.pallas_call(
        functools.partial(_flash_kernel, tk=tk, sub=min(tq, 256)),
        grid=(MLA_HEADS, s // tq),
        in_specs=[pl.BlockSpec((tq, QK_PAD), lambda h, i: (i, h)),
                  pl.BlockSpec((s, QK_PAD), lambda h, i: (0, h)),
                  pl.BlockSpec((s, QK_PAD), lambda h, i: (0, h))],
        out_specs=pl.BlockSpec((tq, MLA_V), lambda h, i: (i, h)),
        out_shape=jax.ShapeDtypeStruct((s, MLA_HEADS * MLA_V), BF16),
        scratch_shapes=[pltpu.VMEM((tq, V7X_LANES), F32), pltpu.VMEM((tq, QK_PAD), F32)],
        compiler_params=_cparams(("parallel", "arbitrary"), 56),
        name="flash_attn",
    )(q, k, v)


def _hyb_out_kernel(x_ref, a_ref, o_ref, w_ref, gate_ref, y_ref):
    m = (jnp.dot(a_ref[...], w_ref[:SGU_WIDTH, :], preferred_element_type=F32)
         + jnp.dot(o_ref[...], w_ref[SGU_WIDTH:, :], preferred_element_type=F32))
    y_ref[...] = x_ref[...] + gate_ref[...] * m


def _hyb_out(x2, a, o, w, gate):
    s, d = x2.shape
    tm = min(s, 512)
    return pl.pallas_call(
        _hyb_out_kernel,
        grid=(s // tm,),
        in_specs=[pl.BlockSpec((tm, d), lambda i: (i, 0)),
                  pl.BlockSpec((tm, a.shape[1]), lambda i: (i, 0)),
                  pl.BlockSpec((tm, o.shape[1]), lambda i: (i, 0)),
                  _const_spec(w.shape),
                  pl.BlockSpec((1, d), lambda i: (0, 0))],
        out_specs=pl.BlockSpec((tm, d), lambda i: (i, 0)),
        out_shape=jax.ShapeDtypeStruct((s, d), F32),
        compiler_params=_cparams(("parallel",), 48),
        name="hyb_out",
    )(x2, a, o, w, gate)


FFN_CHAINS = 2


def _ffn_kernel(x_ref, g_ref, sc_ref, sh_ref, gate_ref, fg_ref, w1_ref, w2_ref, o_ref,
                h_scr, *, final_norm):
    k = pl.program_id(1)

    @pl.when(k == 0)
    def _():
        h_scr[...] = _norm_mod(x_ref[...], g_ref[...], sc_ref[...], sh_ref[...]).astype(BF16)
        o_ref[...] = jnp.zeros(o_ref.shape, F32)

    tk = w1_ref.shape[1]
    half = tk // FFN_CHAINS
    for c in range(FFN_CHAINS):
        cols = slice(c * half, (c + 1) * half)
        hid = jnp.dot(h_scr[...], w1_ref[:, cols], preferred_element_type=F32)
        hid = jnp.square(jnp.maximum(hid, 0.0)).astype(BF16)
        o_ref[...] += jnp.dot(hid, w2_ref[cols, :], preferred_element_type=F32)

    @pl.when(k == pl.num_programs(1) - 1)
    def _():
        y = x_ref[...] + gate_ref[...] * o_ref[...]
        if final_norm:
            y = y * lax.rsqrt(jnp.mean(y * y, axis=-1, keepdims=True) + EPS) * fg_ref[...]
        o_ref[...] = y


def _ffn(x2, g, sc, sh, gate, fg, w1, w2, layer, final_norm):
    s, d = x2.shape
    hidden = w1.shape[2]
    tm = min(s, 512)
    tk = 2048
    vec = pl.BlockSpec((1, d), lambda i, k: (0, 0))
    return pl.pallas_call(
        functools.partial(_ffn_kernel, final_norm=final_norm),
        grid=(s // tm, hidden // tk),
        in_specs=[pl.BlockSpec((tm, d), lambda i, k: (i, 0)), vec, vec, vec, vec, vec,
                  pl.BlockSpec((None, d, tk), lambda i, k: (layer, 0, k)),
                  pl.BlockSpec((None, tk, d), lambda i, k: (layer, k, 0))],
        out_specs=pl.BlockSpec((tm, d), lambda i, k: (i, 0)),
        out_shape=jax.ShapeDtypeStruct((s, d), F32),
        scratch_shapes=[pltpu.VMEM((tm, d), BF16)],
        compiler_params=_cparams(("parallel", "arbitrary"), 60),
        name="ffn",
    )(x2, g, sc, sh, gate, fg, w1, w2)


SSM_IN_HALO = V7X_BF16_SUBLANE_TILE
SSM_IN_SUB = 256
SSM_IN_ROWS = 1024


def _ssm_in_kernel(xp_ref, x_ref, xn_ref, g_ref, sc_ref, sh_ref, w_ref, wdt_ref, dtb_ref,
                   cw_ref, cb_ref, z_ref, xbc_ref, dtt_ref, h_scr, *, nz):
    i = pl.program_id(0)
    j = pl.program_id(1)
    tm = x_ref.shape[0]
    halo = xp_ref.shape[0]
    pad = SSM_CONV // 2

    @pl.when(j == 0)
    def _():
        norm = lambda v: _norm_mod(v, g_ref[...], sc_ref[...], sh_ref[...])
        hb = norm(x_ref[...]).astype(BF16)
        h_scr[0:halo, :] = jnp.where(i == 0, 0.0, norm(xp_ref[...])).astype(BF16)
        h_scr[halo:halo + tm, :] = hb
        h_scr[halo + tm:, :] = jnp.where(i == pl.num_programs(0) - 1, 0.0, norm(xn_ref[...])).astype(BF16)
        r = jnp.dot(hb, wdt_ref[...], preferred_element_type=F32) + dtb_ref[...]
        dtt_ref[...] = (jnp.maximum(r, 0.0) + jnp.log1p(jnp.exp(-jnp.abs(r)))).T

    @pl.when(j < nz)
    def _():
        z_ref[...] = jnp.dot(h_scr[halo:halo + tm, :], w_ref[...],
                             preferred_element_type=F32).astype(BF16)

    @pl.when(j >= nz)
    def _():
        rb = min(tm, SSM_IN_ROWS)
        for r in range(tm // rb):
            for c in range(w_ref.shape[1] // SSM_IN_SUB):
                cols = slice(c * SSM_IN_SUB, (c + 1) * SSM_IN_SUB)
                pr = jnp.dot(h_scr[r * rb:r * rb + rb + 2 * halo, :], w_ref[:, cols],
                             preferred_element_type=F32)
                acc = jnp.broadcast_to(cb_ref[:, cols], (rb, SSM_IN_SUB))
                for t in range(SSM_CONV):
                    acc = acc + cw_ref[t:t + 1, cols] * pr[halo - pad + t:halo - pad + t + rb, :]
                y = _silu(acc)
                for u in range(SSM_IN_SUB // V7X_LANES):
                    xbc_ref[c * (SSM_IN_SUB // V7X_LANES) + u, r * rb:(r + 1) * rb, :] = (
                        y[:, u * V7X_LANES:(u + 1) * V7X_LANES].astype(BF16))


def _ssm_in(x2, g, sc, sh, w_in, dt_bias, conv_w, conv_b):
    s, d = x2.shape
    ndt = 2 * SSM_HEADS
    tm = min(s, 1024)
    tn = 1024
    halo = SSM_IN_HALO
    nz = SSM_INNER // tn
    nrow_h = s // halo
    vec = pl.BlockSpec((1, d), lambda i, j: (0, 0))
    ccol = lambda j: jnp.maximum(j - nz, 0)
    return pl.pallas_call(
        functools.partial(_ssm_in_kernel, nz=nz),
        grid=(s // tm, ZX_WIDTH // tn),
        in_specs=[pl.BlockSpec((halo, d), lambda i, j: (jnp.maximum(i * (tm // halo) - 1, 0), 0)),
                  pl.BlockSpec((tm, d), lambda i, j: (i, 0)),
                  pl.BlockSpec((halo, d), lambda i, j: (jnp.minimum((i + 1) * (tm // halo), nrow_h - 1), 0)),
                  vec, vec, vec,
                  pl.BlockSpec((d, tn), lambda i, j: (0, j)),
                  pl.BlockSpec((d, ndt), lambda i, j: (0, ZX_WIDTH // ndt)),
                  pl.BlockSpec((1, ndt), lambda i, j: (0, 0)),
                  pl.BlockSpec((SSM_CONV, tn), lambda i, j: (0, ccol(j))),
                  pl.BlockSpec((1, tn), lambda i, j: (0, ccol(j)))],
        out_specs=[pl.BlockSpec((tm, tn), lambda i, j: (i, jnp.minimum(j, nz - 1))),
                   pl.BlockSpec((tn // V7X_LANES, tm, V7X_LANES), lambda i, j: (ccol(j), i, 0)),
                   pl.BlockSpec((ndt, tm), lambda i, j: (0, i))],
        out_shape=[jax.ShapeDtypeStruct((s, SSM_INNER), BF16),
                   jax.ShapeDtypeStruct((XBC_BLOCKS, s, V7X_LANES), BF16),
                   jax.ShapeDtypeStruct((ndt, s), F32)],
        scratch_shapes=[pltpu.VMEM((tm + 2 * halo, d), BF16)],
        compiler_params=_cparams(("parallel", "arbitrary"), 48),
        name="ssm_in",
    )(x2, x2, x2, g, sc, sh, w_in, w_in, dt_bias, conv_w, conv_b)


def _chunk_cumsum(dta, tri):
    n = dta.shape[0]
    hi = dta.astype(BF16)
    r1 = dta - hi.astype(F32)
    mid = r1.astype(BF16)
    lo = (r1 - mid.astype(F32)).astype(BF16)
    cs = jnp.dot(jnp.concatenate([hi, mid, lo], axis=0), tri, preferred_element_type=F32)
    return cs[0:n] + cs[n:2 * n] + cs[2 * n:3 * n]


def _lane_bcast_col(row, L):
    return jnp.broadcast_to(row, (V7X_LANES, L)).T


def _ssd_kernel(xa_ref, ba_ref, ca_ref, dta_ref, xb_ref, bb_ref, cb_ref, dtb_ref, alog_ref, d_ref,
                ya_ref, yb_ref, stf_scr, stb_scr):
    L = xa_ref.shape[1]
    P = SSM_HEAD_DIM
    H = SSM_HEADS

    @pl.when(pl.program_id(0) == 0)
    def _():
        stf_scr[...] = jnp.zeros(stf_scr.shape, F32)
        stb_scr[...] = jnp.zeros(stb_scr.shape, F32)

    neg_a = -jnp.exp(alog_ref[...])
    dt_a = dta_ref[...]
    dt_b = dtb_ref[H:, :]
    kk = lax.broadcasted_iota(jnp.int32, (L, L), 0)
    ii = lax.broadcasted_iota(jnp.int32, (L, L), 1)
    tri_f = jnp.where(kk <= ii, 1.0, 0.0).astype(BF16)
    tri_b = jnp.where(kk >= ii, 1.0, 0.0).astype(BF16)
    cf = _chunk_cumsum(dt_a[:H] * neg_a[:H], tri_f) * LOG2E
    rc = _chunk_cumsum(jnp.concatenate([dt_a[H:] * neg_a[H:], dt_b * neg_a[H:]], axis=0), tri_b) * LOG2E
    ra = rc[:H]
    rb = rc[H:]
    cf_dt = cf - jnp.log(dt_a[:H]) * LOG2E
    ra_dt = ra - jnp.log(dt_a[H:]) * LOG2E
    rb_dt = rb - jnp.log(dt_b) * LOG2E
    f_end = cf[:, L - 1:L]
    b_end = rb[:, 0:1]
    to_end_f = jnp.exp2(f_end - cf_dt)
    to_end_b = jnp.exp2(b_end - rb_dt)
    dec_f = jnp.exp2(f_end)
    dec_b = jnp.exp2(b_end)
    below = kk > ii
    diag = kk == ii
    dt_sum = jnp.log(dt_a[:H] + dt_a[H:]) * LOG2E
    low =lax.broadcasted_iota(jnp.int32, (1, V7X_LANES), 1) < P
    heads_per_block = V7X_LANES // P
    reps = (1, L // V7X_LANES)

    for g in range(SSM_GROUPS):
        gl = slice(g * HEADS_PER_GROUP * P, (g + 1) * HEADS_PER_GROUP * P)
        cga = ca_ref[g]
        bta = ba_ref[g].astype(F32).T.astype(BF16)
        cba = jnp.dot(cga, bta, preferred_element_type=F32).astype(BF16)
        stf = stf_scr[:, gl]
        ysf = jnp.dot(cga, stf.astype(BF16), preferred_element_type=F32)
        cgb = cb_ref[g]
        btb = bb_ref[g].astype(F32).T.astype(BF16)
        stb = stb_scr[:, gl]
        ysb = jnp.dot(cgb, stb.astype(BF16), preferred_element_type=F32)
        for q in range(PAIRS_PER_GROUP):
            blk = g * PAIRS_PER_GROUP + q
            xpa = xa_ref[blk]
            xpb = xb_ref[blk]
            lanes = slice(q * V7X_LANES, (q + 1) * V7X_LANES)
            yas, ybs, sfs, sbs = [], [], [], []
            for e in range(heads_per_block):
                hd = blk * heads_per_block + e
                hr = slice(hd, hd + 1)
                cf_col = _lane_bcast_col(cf[hr, :], L)
                ra_col = _lane_bcast_col(ra[hr, :], L)
                rb_col = _lane_bcast_col(rb[hr, :], L)
                arg = jnp.where(below, jnp.tile(cf_col, reps) - cf_dt[hr, :],
                                jnp.where(diag, dt_sum[hr, :], jnp.tile(ra_col, reps) - ra_dt[hr, :]))
                w = jnp.exp2(arg).astype(BF16) * cba
                yh = jnp.dot(w, xpa, preferred_element_type=F32)
                yas.append(yh + ysf[:, lanes] * jnp.exp2(cf_col))
                ybs.append(ysb[:, lanes] * jnp.exp2(rb_col))
                sfs.append(jnp.dot(bta * to_end_f[hr, :].astype(BF16), xpa, preferred_element_type=F32))
                sbs.append(jnp.dot(btb * to_end_b[hr, :].astype(BF16), xpb, preferred_element_type=F32))
            sl = slice(blk * V7X_LANES, (blk + 1) * V7X_LANES)
            ya = jnp.where(low, yas[0], yas[1]) + d_ref[:, sl] * xpa.astype(F32)
            ya_ref[blk] = ya.astype(BF16)
            yb_ref[blk] = jnp.where(low, ybs[0], ybs[1]).astype(BF16)
            h0 = blk * heads_per_block
            decf = jnp.where(low, dec_f[h0:h0 + 1, :], dec_f[h0 + 1:h0 + 2, :])
            decb = jnp.where(low, dec_b[h0:h0 + 1, :], dec_b[h0 + 1:h0 + 2, :])
            stf_scr[:, sl] = stf[:, lanes] * decf + jnp.where(low, sfs[0], sfs[1])
            stb_scr[:, sl] = stb[:, lanes] * decb + jnp.where(low, sbs[0], sbs[1])


def _ssd_scan(xbc_t, dtt, alog_col, d_full):
    s = xbc_t.shape[1]
    L = SSM_CHUNK
    nc = s // L
    fwd = lambda t: t
    bwd = lambda t: nc - 1 - t
    b0 = X_BLOCKS // SSM_GROUPS
    c0 = (X_BLOCKS + SSM_GROUPS) // SSM_GROUPS

    def chunk_specs(cidx):
        return [pl.BlockSpec((X_BLOCKS, L, V7X_LANES), lambda t: (0, cidx(t), 0)),
                pl.BlockSpec((SSM_GROUPS, L, V7X_LANES), lambda t: (b0, cidx(t), 0)),
                pl.BlockSpec((SSM_GROUPS, L, V7X_LANES), lambda t: (c0, cidx(t), 0)),
                pl.BlockSpec((2 * SSM_HEADS, L), lambda t: (0, cidx(t)))]

    yspec = lambda cidx: pl.BlockSpec((X_BLOCKS, L, V7X_LANES), lambda t: (0, cidx(t), 0))
    yshape = jax.ShapeDtypeStruct((X_BLOCKS, s, V7X_LANES), BF16)
    return pl.pallas_call(
        _ssd_kernel,
        grid=(nc,),
        in_specs=chunk_specs(fwd) + chunk_specs(bwd)
        + [pl.BlockSpec((2 * SSM_HEADS, 1), lambda t: (0, 0)),
           pl.BlockSpec((1, SSM_INNER), lambda t: (0, 0))],
        out_specs=[yspec(fwd), yspec(bwd)],
        out_shape=[yshape, yshape],
        scratch_shapes=[pltpu.VMEM((SSM_STATE, SSM_INNER), F32), pltpu.VMEM((SSM_STATE, SSM_INNER), F32)],
        compiler_params=_cparams(("arbitrary",), 48),
        name="ssd_bidir",
    )(xbc_t, xbc_t, xbc_t, dtt, xbc_t, xbc_t, xbc_t, dtt, alog_col, d_full)


def _ssm_gate_norm(g, ya_ref, yb_ref, z_ref, ng_ref):
    per_group = PAIRS_PER_GROUP
    gw = SSM_INNER // SSM_GROUPS
    ys = []
    ss = None
    for p in range(per_group):
        j = g * per_group + p
        lanes = slice(j * V7X_LANES, (j + 1) * V7X_LANES)
        y = ya_ref[j].astype(F32) + yb_ref[j].astype(F32)
        y = y * _silu(z_ref[:, lanes].astype(F32))
        ys.append(y)
        t = jnp.sum(y * y, axis=-1, keepdims=True)
        ss = t if ss is None else ss + t
    scale = lax.rsqrt(ss * (1.0 / gw) + EPS)
    pieces = []
    for p in range(per_group):
        j = g * per_group + p
        lanes = slice(j * V7X_LANES, (j + 1) * V7X_LANES)
        pieces.append((ys[p] * scale * ng_ref[:, lanes]).astype(BF16))
    return jnp.concatenate(pieces, axis=1)


def _ssm_out_kernel(x_ref, ya_ref, yb_ref, z_ref, ng_ref, w_ref, gate_ref, o_ref):
    gw = SSM_INNER // SSM_GROUPS
    acc = None
    for g in range(SSM_GROUPS):
        yn = _ssm_gate_norm(g, ya_ref, yb_ref, z_ref, ng_ref)
        part = jnp.dot(yn, w_ref[g * gw:(g + 1) * gw, :], preferred_element_type=F32)
        acc = part if acc is None else acc + part
    o_ref[...] = x_ref[...] + gate_ref[...] * acc


def _ssm_out(x2, ya, yb, z, ng, w, gate):
    s, d = x2.shape
    tm = min(s, 256)
    yspec = pl.BlockSpec((X_BLOCKS, tm, V7X_LANES), lambda i: (0, i, 0))
    return pl.pallas_call(
        _ssm_out_kernel,
        grid=(s // tm,),
        in_specs=[pl.BlockSpec((tm, d), lambda i: (i, 0)), yspec, yspec,
                  pl.BlockSpec((tm, SSM_INNER), lambda i: (i, 0)),
                  pl.BlockSpec((1, SSM_INNER), lambda i: (0, 0)),
                  _const_spec(w.shape),
                  pl.BlockSpec((1, d), lambda i: (0, 0))],
        out_specs=pl.BlockSpec((tm, d), lambda i: (i, 0)),
        out_shape=jax.ShapeDtypeStruct((s, d), F32),
        compiler_params=_cparams(("parallel",), 56),
        name="ssm_out",
    )(x2, ya, yb, z, ng, w, gate)


def _rot_cols(w):
    half = MLA_ROPE // 2
    return jnp.concatenate([-w[..., half:], w[..., :half]], axis=-1)


def _pad_lanes(w, width):
    return jnp.pad(w, [(0, 0)] * (w.ndim - 1) + [(0, width - w.shape[-1])])


def kernel(x, c, positions, ada_w, ada_b, norm_mix_g, norm_ffn_g, ffn_w1, ffn_w2, hyb_w_in, sgu_norm_g, sgu_w, sgu_b, mla_q_norm_g, mla_kv_norm_g, mla_w_uq, mla_w_ukv, hyb_w_out, ssm_w_in, ssm_conv_w, ssm_conv_b, ssm_dt_bias, ssm_a_log, ssm_d, ssm_norm_g, ssm_w_out, final_norm_g):
    batch, s, d = x.shape
    assert batch == 1 and d == D_MODEL and s % 1024 == 0
    depth = ada_w.shape[0]
    assert depth == 2
    x2 = x.reshape(s, d)

    mod = _ada_mod(c, ada_w, ada_b)
    mods = [[mod[l, :, i * d:(i + 1) * d] for i in range(6)] for l in range(depth)]
    row = lambda v: v.reshape(1, -1)

    sh1, sc1, g1, sh2, sc2, g2 = mods[0]
    w_in = hyb_w_in[0]
    c_pe = 2 * SGU_WIDTH + MLA_Q_RANK + MLA_KV_RANK
    w_kpe = w_in[:, c_pe:]
    win_ext = jnp.concatenate(
        [w_in[:, :c_pe], _pad_lanes(w_kpe, V7X_LANES), _pad_lanes(_rot_cols(w_kpe), V7X_LANES)],
        axis=1).astype(BF16)
    wq = mla_w_uq[0].reshape(MLA_Q_RANK, MLA_HEADS, MLA_NOPE + MLA_ROPE)
    wqa = _pad_lanes(wq, QK_PAD).reshape(MLA_Q_RANK, MLA_HEADS * QK_PAD).astype(BF16)
    wqb = _pad_lanes(_rot_cols(wq[..., MLA_NOPE:]), V7X_LANES).reshape(
        MLA_Q_RANK, MLA_HEADS * V7X_LANES).astype(BF16)
    wkv = mla_w_ukv[0].reshape(MLA_KV_RANK, MLA_HEADS, MLA_NOPE + MLA_V)
    wk = wkv[..., :MLA_NOPE].reshape(MLA_KV_RANK, MLA_HEADS * MLA_NOPE).astype(BF16)
    wv = wkv[..., MLA_NOPE:].reshape(MLA_KV_RANK, MLA_HEADS * MLA_V).astype(BF16)
    sb_full = jnp.repeat(sgu_b[0].T, SGU_HEAD, axis=1)

    half = MLA_ROPE // 2
    inv_freq = ROPE_THETA ** (-jnp.arange(half, dtype=F32) / half)
    inv128 = jnp.tile(inv_freq, V7X_LANES // half).reshape(1, V7X_LANES)
    cos, sin = _rope_tables(positions.astype(F32).reshape(s, 1), inv128)
    q_scale = (MLA_NOPE + MLA_ROPE) ** -0.5 * LOG2E

    a_out, q, k, v = _hyb_in(x2, row(norm_mix_g[0]), sc1, sh1, win_ext, row(sgu_norm_g[0]),
                             sgu_w[0].astype(BF16), sb_full, row(mla_q_norm_g[0]),
                             row(mla_kv_norm_g[0]), wqa, wqb, wk, wv, cos, sin, q_scale)
    o = _flash(q, k, v)
    x2 = _hyb_out(x2, a_out, o, hyb_w_out[0].astype(BF16), g1)
    w1_bf = ffn_w1.astype(BF16)
    w2_bf = ffn_w2.astype(BF16)
    x2 = _ffn(x2, row(norm_ffn_g[0]), sc2, sh2, g2, row(final_norm_g), w1_bf, w2_bf, 0, False)

    sh1, sc1, g1, sh2, sc2, g2 = mods[1]
    z, xbc_t, dtt = _ssm_in(x2, row(norm_mix_g[1]), sc1, sh1, ssm_w_in[0].astype(BF16),
                            ssm_dt_bias[0].reshape(1, 2 * SSM_HEADS),
                            ssm_conv_w[0], row(ssm_conv_b[0]))
    alog_col = ssm_a_log[0].reshape(2 * SSM_HEADS, 1)
    d_full = jnp.repeat(ssm_d[0], SSM_HEAD_DIM).reshape(1, SSM_INNER)
    ya, yb = _ssd_scan(xbc_t, dtt, alog_col, d_full)
    x2 = _ssm_out(x2, ya, yb, z, row(ssm_norm_g[0]), ssm_w_out[0].astype(BF16), g1)
    x2 = _ffn(x2, row(norm_ffn_g[1]), sc2, sh2, g2, row(final_norm_g), w1_bf, w2_bf, 1, True)
    return x2.reshape(batch, s, d)
```
